```python
import math, functools
import jax, jax.numpy as jnp
from jax import lax
import numpy as np


D_MODEL = 1024
BATCH = 32
SEQ = 2048
DEPTH = 2
DEC_BATCH = 128
DEC_SEQ = 4
PAST_LEN = 16384
PAGE_SIZE = 128

D_MIX = D_MODEL
POOL_WINDOWS = (2, 4, 8, 16)
N_POOL_GROUPS = 4
POOL_GROUP = D_MIX // 16
D_POOL = N_POOL_GROUPS * POOL_GROUP
POOL_STATE = max(POOL_WINDOWS) - 1
MLSTM_HEADS = 4
MLSTM_DH = D_MIX // 16
D_MLSTM = MLSTM_HEADS * MLSTM_DH
MLSTM_CHUNK = 64
MLA_HEADS = 8
QK_NOPE = D_MIX // 16
QK_ROPE = QK_NOPE // 2
V_DIM = D_MIX // 16
D_MLA = MLA_HEADS * V_DIM
Q_LORA = 3 * D_MODEL // 8
KV_LORA = D_MODEL // 4
ROPE_THETA = 10000.0
ATTN_SCALE = (QK_NOPE + QK_ROPE) ** -0.5
Q_BLOCK = 128
D_FF = -(-8 * D_MODEL // (3 * 256)) * 256
D_PLE = 256
EPS = 1e-6
SPLIT_SIZES = (D_POOL, D_MLSTM, D_MLSTM, D_MLSTM, D_MLSTM, MLSTM_HEADS, MLSTM_HEADS, Q_LORA, KV_LORA, QK_ROPE)
D_IN = sum(SPLIT_SIZES)

kernel_name = "hymba_pool_mlstm_mla_decoder_step"


def rmsnorm(x, g):
    xf = x.astype(jnp.float32)
    y = xf * lax.rsqrt(jnp.mean(xf * xf, axis=-1, keepdims=True) + EPS)
    return (y * g.astype(jnp.float32)).astype(x.dtype)


def rope_cos_sin(pos):
    inv = jnp.power(ROPE_THETA, -jnp.arange(0, QK_ROPE, 2, dtype=jnp.float32) / QK_ROPE)
    ang = pos.astype(jnp.float32)[:, None] * inv[None, :]
    return jnp.cos(ang), jnp.sin(ang)


def apply_rope(x, cos, sin):
    x1, x2 = jnp.split(x.astype(jnp.float32), 2, axis=-1)
    return jnp.concatenate([x1 * cos - x2 * sin, x2 * cos + x1 * sin], axis=-1).astype(x.dtype)


def pool_mixer(u, prefix, pos, pool_w, pool_scale):
    T = u.shape[1]
    P = POOL_STATE
    full = jnp.concatenate([prefix.astype(u.dtype), u], axis=1)
    cs = jnp.cumsum(full.astype(jnp.float32), axis=1)
    cs = jnp.concatenate([jnp.zeros_like(cs[:, :1]), cs], axis=1)
    outs = []
    for g, w in enumerate(POOL_WINDOWS):
        sl = slice(g * POOL_GROUP, (g + 1) * POOL_GROUP)
        win = cs[:, P + 1:P + T + 1, sl] - cs[:, P + 1 - w:P + T + 1 - w, sl]
        cnt = jnp.minimum(pos + 1, w).astype(jnp.float32)
        d = (win / cnt[None, :, None] - u[..., sl].astype(jnp.float32)).astype(u.dtype)
        outs.append(jnp.einsum('btc,cd->btd', d, pool_w[g]))
    y = jnp.concatenate(outs, axis=-1) * pool_scale
    return y.astype(u.dtype), full[:, -P:]


def mlstm_chunk_step(carry, xs):
    C, n, m = carry
    q, k, v, ig, lf = xs
    L = q.shape[1]
    b = jnp.cumsum(lf, axis=1)
    causal = jnp.tril(jnp.ones((L, L), dtype=bool))[None, :, :, None]
    logw = jnp.where(causal, b[:, :, None, :] - b[:, None, :, :] + ig[:, None, :, :], -jnp.inf)
    inter = b + m[:, None, :]
    m_t = jnp.maximum(inter, jnp.max(logw, axis=2))
    w = jnp.exp(logw - m_t[:, :, None, :])
    a = jnp.exp(inter - m_t)
    s = jnp.einsum('bthd,bshd->btsh', q, k) * w
    num = jnp.einsum('btsh,bshe->bthe', s, v) + a[..., None] * jnp.einsum('bthd,bhde->bthe', q, C)
    den = jnp.sum(s, axis=2) + a * jnp.einsum('bthd,bhd->bth', q, n)
    h = num / jnp.maximum(jnp.abs(den), jnp.exp(-m_t))[..., None]
    m_new = m_t[:, -1]
    carry_decay = jnp.exp(b[:, -1] + m - m_new)
    wk = jnp.exp(b[:, -1:] - b + ig - m_new[:, None])
    C_new = carry_decay[..., None, None] * C + jnp.einsum('bsh,bshd,bshe->bhde', wk, k, v)
    n_new = carry_decay[..., None] * n + jnp.einsum('bsh,bshd->bhd', wk, k)
    return (C_new, n_new, m_new), h


def mlstm(q, k, v, ig, lf, C0, n0, m0):
    B, T, H, DH = q.shape
    L = math.gcd(T, MLSTM_CHUNK)
    NC = T // L
    def chunks(a):
        return jnp.swapaxes(a.reshape((B, NC, L) + a.shape[2:]), 0, 1)
    (C, n, m), h = lax.scan(mlstm_chunk_step, (C0, n0, m0), (chunks(q), chunks(k), chunks(v), chunks(ig), chunks(lf)))
    return jnp.swapaxes(h, 0, 1).reshape(B, T, H, DH), C, n, m


def mla_prompt(q_lat, q_rope, c, kr):
    B, S, H, _ = q_lat.shape
    nb = S // Q_BLOCK
    qb = jnp.swapaxes(q_lat.reshape(B, nb, Q_BLOCK, H, KV_LORA), 0, 1)
    rb = jnp.swapaxes(q_rope.reshape(B, nb, Q_BLOCK, H, QK_ROPE), 0, 1)
    kpos = jnp.arange(S)
    def block(args):
        ql, qr, i = args
        s = (jnp.einsum('bqhc,bkc->bhqk', ql, c) + jnp.einsum('bqhr,bkr->bhqk', qr, kr)).astype(jnp.float32) * ATTN_SCALE
        qpos = i * Q_BLOCK + jnp.arange(Q_BLOCK)
        s = jnp.where(kpos[None, :] <= qpos[:, None], s, -jnp.inf)
        p = jax.nn.softmax(s, axis=-1).astype(c.dtype)
        return jnp.einsum('bhqk,bkc->bqhc', p, c)
    o = lax.map(block, (qb, rb, jnp.arange(nb)))
    return jnp.swapaxes(o, 0, 1).reshape(B, S, H, KV_LORA)


def mla_sample(q_lat, q_rope, c_new, kr_new, cache_latent, cache_krope, page_table, layer):
    DB, T, H, _ = q_lat.shape
    ql = q_lat.astype(jnp.float32)
    qr = q_rope.astype(jnp.float32)
    def step(carry, pages):
        m, l, acc = carry
        cl = cache_latent[layer, pages].astype(jnp.float32)
        ck = cache_krope[layer, pages].astype(jnp.float32)
        s = (jnp.einsum('bqhc,bpc->bhqp', ql, cl) + jnp.einsum('bqhr,bpr->bhqp', qr, ck)) * ATTN_SCALE
        m_new = jnp.maximum(m, jnp.max(s, axis=-1))
        alpha = jnp.exp(m - m_new)
        pe = jnp.exp(s - m_new[..., None])
        return (m_new, alpha * l + jnp.sum(pe, axis=-1), alpha[..., None] * acc + jnp.einsum('bhqp,bpc->bhqc', pe, cl)), None
    init = (jnp.full((DB, H, T), -jnp.inf, jnp.float32), jnp.zeros((DB, H, T), jnp.float32), jnp.zeros((DB, H, T, KV_LORA), jnp.float32))
    (m, l, acc), _ = lax.scan(step, init, page_table.T)
    cn = c_new.astype(jnp.float32)
    s = (jnp.einsum('bqhc,bkc->bhqk', ql, cn) + jnp.einsum('bqhr,bkr->bhqk', qr, kr_new.astype(jnp.float32))) * ATTN_SCALE
    s = jnp.where(jnp.tril(jnp.ones((T, T), dtype=bool)), s, -jnp.inf)
    m_fin = jnp.maximum(m, jnp.max(s, axis=-1))
    alpha = jnp.exp(m - m_fin)
    pe = jnp.exp(s - m_fin[..., None])
    l = alpha * l + jnp.sum(pe, axis=-1)
    acc = alpha[..., None] * acc + jnp.einsum('bhqk,bkc->bhqc', pe, cn)
    return jnp.swapaxes(acc / l[..., None], 1, 2).astype(q_lat.dtype)


def mixing(h, pos, pool_prefix, C0, n0, m0, attend, w_in, b_igate, b_fgate, pool_w, pool_scale, mlstm_norm, q_norm, w_uq, kv_norm, w_uk, w_uv, w_out):
    B, T, _ = h.shape
    z = jnp.einsum('btd,de->bte', h, w_in)
    idx = np.cumsum(SPLIT_SIZES)[:-1].tolist()
    u_pool, q, k, v, o_pre, i_pre, f_pre, c_q, c_kv, k_r = jnp.split(z, idx, axis=-1)
    y_pool, pool_state = pool_mixer(u_pool, pool_prefix, pos, pool_w, pool_scale)
    def heads(a):
        return a.reshape(B, T, MLSTM_HEADS, MLSTM_DH).astype(jnp.float32)
    ig = (i_pre + b_igate).astype(jnp.float32)
    lf = jax.nn.log_sigmoid((f_pre + b_fgate).astype(jnp.float32))
    hm, C, n, m = mlstm(heads(q), heads(k) * MLSTM_DH ** -0.5, heads(v), ig, lf,
                        C0.astype(jnp.float32), n0.astype(jnp.float32), m0.astype(jnp.float32))
    hm = rmsnorm(hm, mlstm_norm.reshape(MLSTM_HEADS, MLSTM_DH)).reshape(B, T, D_MLSTM)
    y_mlstm = (hm * jax.nn.sigmoid(o_pre.astype(jnp.float32))).astype(h.dtype)
    cos, sin = rope_cos_sin(pos)
    qf = jnp.einsum('btc,chr->bthr', rmsnorm(c_q, q_norm), w_uq)
    q_nope = qf[..., :QK_NOPE]
    q_rope = apply_rope(qf[..., QK_NOPE:], cos[:, None, :], sin[:, None, :])
    ckv = rmsnorm(c_kv, kv_norm)
    kr = apply_rope(k_r, cos, sin)
    q_lat = jnp.einsum('bthd,chd->bthc', q_nope, w_uk)
    o_lat = attend(q_lat, q_rope, ckv, kr)
    y_mla = jnp.einsum('bthc,chd->bthd', o_lat, w_uv).reshape(B, T, D_MLA)
    y = jnp.concatenate([y_pool, y_mlstm, y_mla.astype(h.dtype)], axis=-1)
    return jnp.einsum('bte,ed->btd', y, w_out), pool_state, C, n, m, ckv, kr


def decoder_layer(x, p, pos, pool_prefix, C0, n0, m0, attend, norm_mix, w_in, b_igate, b_fgate, pool_w, pool_scale, mlstm_norm, q_norm, w_uq, kv_norm, w_uk, w_uv, w_out, norm_ffn, w_gate, w_up, w_down, ple_norm, w_ple_gate, w_ple_proj):
    mix, pool_state, C, n, m, ckv, kr = mixing(rmsnorm(x, norm_mix), pos, pool_prefix, C0, n0, m0, attend, w_in, b_igate, b_fgate, pool_w, pool_scale, mlstm_norm, q_norm, w_uq, kv_norm, w_uk, w_uv, w_out)
    x = x + mix
    hf = rmsnorm(x, norm_ffn)
    x = x + jnp.einsum('btf,fd->btd', jax.nn.silu(jnp.einsum('btd,df->btf', hf, w_gate)) * jnp.einsum('btd,df->btf', hf, w_up), w_down)
    gate = jax.nn.sigmoid(jnp.einsum('btd,de->bte', rmsnorm(x, ple_norm), w_ple_gate))
    x = x + gate * jnp.einsum('btp,pd->btd', p, w_ple_proj)
    return x, (ckv, kr, pool_state, C, n, m)


def setup_inputs(seed: int = 0) -> dict:
    key = jax.random.key(seed)
    ks = list(jax.random.split(key, 40))
    def nrm(i, shape, scale):
        return jax.random.normal(ks[i], shape, jnp.float32) * scale
    n_pages = PAST_LEN // PAGE_SIZE
    n_used = DEC_BATCH * n_pages
    n_phys = n_used + max(1, n_used // 4)
    page_table = jax.random.permutation(ks[0], n_phys)[:n_used].reshape(DEC_BATCH, n_pages).astype(jnp.int32)
    return {
        "x_prompt": nrm(1, (BATCH, SEQ, D_MODEL), 1.0),
        "x_sample": nrm(2, (DEC_BATCH, DEC_SEQ, D_MODEL), 1.0),
        "cache_latent": nrm(3, (DEPTH, n_phys, PAGE_SIZE, KV_LORA), 1.0),
        "cache_krope": nrm(4, (DEPTH, n_phys, PAGE_SIZE, QK_ROPE), 1.0),
        "state_pool": nrm(5, (DEPTH, DEC_BATCH, POOL_STATE, D_POOL), 1.0),
        "state_C": nrm(6, (DEPTH, DEC_BATCH, MLSTM_HEADS, MLSTM_DH, MLSTM_DH), 0.5),
        "state_n": nrm(7, (DEPTH, DEC_BATCH, MLSTM_HEADS, MLSTM_DH), 0.5),
        "state_m": nrm(8, (DEPTH, DEC_BATCH, MLSTM_HEADS), 1.0),
        "page_table": page_table,
        "p_prompt": nrm(9, (DEPTH, BATCH, SEQ, D_PLE), 1.0),
        "p_sample": nrm(10, (DEPTH, DEC_BATCH, DEC_SEQ, D_PLE), 1.0),
        "norm_mix": 1.0 + nrm(11, (DEPTH, D_MODEL), 0.02),
        "w_in": nrm(12, (DEPTH, D_MODEL, D_IN), D_MODEL ** -0.5),
        "b_igate": nrm(13, (DEPTH, MLSTM_HEADS), 0.1),
        "b_fgate": 3.0 + nrm(14, (DEPTH, MLSTM_HEADS), 0.1),
        "pool_w": nrm(15, (DEPTH, N_POOL_GROUPS, POOL_GROUP, POOL_GROUP), POOL_GROUP ** -0.5),
        "pool_scale": 1.0 + nrm(16, (DEPTH, D_POOL), 0.02),
        "mlstm_norm": 1.0 + nrm(17, (DEPTH, D_MLSTM), 0.02),
        "q_norm": 1.0 + nrm(18, (DEPTH, Q_LORA), 0.02),
        "w_uq": nrm(19, (DEPTH, Q_LORA, MLA_HEADS, QK_NOPE + QK_ROPE), Q_LORA ** -0.5),
        "kv_norm": 1.0 + nrm(20, (DEPTH, KV_LORA), 0.02),
        "w_uk": nrm(21, (DEPTH, KV_LORA, MLA_HEADS, QK_NOPE), KV_LORA ** -0.5),
        "w_uv": nrm(22, (DEPTH, KV_LORA, MLA_HEADS, V_DIM), KV_LORA ** -0.5),
        "w_out": nrm(23, (DEPTH, D_MIX, D_MODEL), D_MIX ** -0.5),
        "norm_ffn": 1.0 + nrm(24, (DEPTH, D_MODEL), 0.02),
        "w_gate": nrm(25, (DEPTH, D_MODEL, D_FF), D_MODEL ** -0.5),
        "w_up": nrm(26, (DEPTH, D_MODEL, D_FF), D_MODEL ** -0.5),
        "w_down": nrm(27, (DEPTH, D_FF, D_MODEL), D_FF ** -0.5),
        "ple_norm": 1.0 + nrm(28, (DEPTH, D_MODEL), 0.02),
        "w_ple_gate": nrm(29, (DEPTH, D_MODEL, D_MODEL), D_MODEL ** -0.5),
        "w_ple_proj": nrm(30, (DEPTH, D_PLE, D_MODEL), D_PLE ** -0.5),
        "final_norm": 1.0 + nrm(31, (D_MODEL,), 0.02),
    }


def reference(x_prompt, x_sample, cache_latent, cache_krope, state_pool, state_C, state_n, state_m, page_table, p_prompt, p_sample, norm_mix, w_in, b_igate, b_fgate, pool_w, pool_scale, mlstm_norm, q_norm, w_uq, kv_norm, w_uk, w_uv, w_out, norm_ffn, w_gate, w_up, w_down, ple_norm, w_ple_gate, w_ple_proj, final_norm):
    pos_p = jnp.arange(SEQ)
    pos_s = PAST_LEN + jnp.arange(DEC_SEQ)
    xp, xs = x_prompt, x_sample
    st_p, st_s = [], []
    for i in range(DEPTH):
        lw = (norm_mix[i], w_in[i], b_igate[i], b_fgate[i], pool_w[i], pool_scale[i], mlstm_norm[i], q_norm[i], w_uq[i], kv_norm[i], w_uk[i], w_uv[i], w_out[i], norm_ffn[i], w_gate[i], w_up[i], w_down[i], ple_norm[i], w_ple_gate[i], w_ple_proj[i])
        xp, sp = decoder_layer(xp, p_prompt[i], pos_p,
                               jnp.zeros((BATCH, POOL_STATE, D_POOL), xp.dtype),
                               jnp.zeros((BATCH, MLSTM_HEADS, MLSTM_DH, MLSTM_DH), jnp.float32),
                               jnp.zeros((BATCH, MLSTM_HEADS, MLSTM_DH), jnp.float32),
                               jnp.zeros((BATCH, MLSTM_HEADS), jnp.float32),
                               mla_prompt, *lw)
        attend_s = functools.partial(mla_sample, cache_latent=cache_latent, cache_krope=cache_krope, page_table=page_table, layer=i)
        xs, ss = decoder_layer(xs, p_sample[i], pos_s, state_pool[i], state_C[i], state_n[i], state_m[i], attend_s, *lw)
        st_p.append(sp)
        st_s.append(ss)
    y_prompt = rmsnorm(xp, final_norm)
    y_sample = rmsnorm(xs, final_norm)
    latent_p = jnp.stack([s[0] for s in st_p])
    krope_p = jnp.stack([s[1] for s in st_p])
    pool_p = jnp.stack([s[2] for s in st_p])
    C_p = jnp.stack([s[3] for s in st_p])
    n_p = jnp.stack([s[4] for s in st_p])
    m_p = jnp.stack([s[5] for s in st_p])
    latent_s = jnp.stack([s[0] for s in st_s])
    krope_s = jnp.stack([s[1] for s in st_s])
    pool_s = jnp.stack([s[2] for s in st_s])
    C_s = jnp.stack([s[3] for s in st_s])
    n_s = jnp.stack([s[4] for s in st_s])
    m_s = jnp.stack([s[5] for s in st_s])
    return (y_prompt, y_sample, latent_p, krope_p, pool_p, C_p, n_p, m_p, latent_s, krope_s, pool_s, C_s, n_s, m_s)
```

```python
import functools

import numpy as np
import jax
import jax.numpy as jnp
from jax import lax
from jax.experimental import pallas as pl
from jax.experimental.pallas import tpu as pltpu

D_MODEL = 1024
POOL_WINDOWS = (2, 4, 8, 16)
POOL_GROUP = 64
D_POOL = 256
POOL_STATE = 15
MLSTM_HEADS = 4
MLSTM_DH = 64
D_MLSTM = 256
MLA_HEADS = 8
QK_NOPE = 64
QK_ROPE = 32
V_DIM = 64
D_MLA = 512
Q_LORA = 384
KV_LORA = 256
ROPE_THETA = 10000.0
ATTN_SCALE = (QK_NOPE + QK_ROPE) ** -0.5
D_FF = 2816
D_PLE = 256
EPS = 1e-6

F32 = jnp.float32
MXU_DTYPE = jnp.bfloat16
POOL_HALO = 16
MLSTM_CHUNK = 128
NEG_BIG = -1e30
VMEM_LIMIT = 56 * 1024 * 1024

_C_QKV = 0
_C_OPRE = 768
_C_UPOOL = 1024
_C_CQ = 1280
_C_CKV = 1664
_C_GATES = 1920
_C_KRT = 2048
_C_END = 2304


def _dot(a, b):
    return jnp.dot(a, b, preferred_element_type=F32)


def _dot_nt(a, b):
    return lax.dot_general(a, b, (((1,), (1,)), ((), ())), preferred_element_type=F32)


def _rms(x, g):
    return x * lax.rsqrt(jnp.mean(x * x, axis=-1, keepdims=True) + EPS) * g


def _tile(n, pref):
    t = min(n, pref)
    while n % t:
        t //= 2
    return t


def _const_spec(shape):
    nd = len(shape)
    return pl.BlockSpec(shape, lambda *_: (0,) * nd, pipeline_mode=pl.Buffered(1))


def _params(sem):
    return pltpu.CompilerParams(dimension_semantics=sem, vmem_limit_bytes=VMEM_LIMIT)


def _inproj_body(x_ref, nrm_ref, w_ref, qn_ref, kvn_ref, wuq_ref, wukp_ref, cos_ref, sin_ref,
                 qkv_ref, opre_ref, upool_ref, gates_ref, qlat_ref, r_ref, kp_ref, ckv_ref, kr_ref):
    hb = _rms(x_ref[...], nrm_ref[...]).astype(MXU_DTYPE)

    def seg(a, b):
        return _dot(hb, w_ref[:, a:b])

    qkv_ref[:, 0:256] = seg(_C_QKV, _C_QKV + 256).astype(qkv_ref.dtype)
    qkv_ref[:, 256:512] = (seg(_C_QKV + 256, _C_QKV + 512) * (MLSTM_DH ** -0.5)).astype(qkv_ref.dtype)
    qkv_ref[:, 512:768] = seg(_C_QKV + 512, _C_QKV + 768).astype(qkv_ref.dtype)
    opre_ref[...] = seg(_C_OPRE, _C_UPOOL)
    upool_ref[...] = seg(_C_UPOOL, _C_CQ)
    gates_ref[...] = seg(_C_GATES, _C_KRT)

    cos = cos_ref[...]
    sin = sin_ref[...]

    cqn = _rms(seg(_C_CQ, _C_CKV), qn_ref[...]).astype(MXU_DTYPE)
    qf = _dot(cqn, wuq_ref[...])
    r1 = qf[:, 512:640]
    r2 = qf[:, 640:768]
    r_ref[:, 0:128] = (r1 * cos - r2 * sin).astype(r_ref.dtype)
    r_ref[:, 128:256] = (r2 * cos + r1 * sin).astype(r_ref.dtype)
    for p in range(MLA_HEADS // 2):
        ql = _dot(qf[:, 128 * p:128 * p + 128].astype(MXU_DTYPE), wukp_ref[p])
        qlat_ref[2 * p] = ql[:, 0:256].astype(qlat_ref.dtype)
        qlat_ref[2 * p + 1] = ql[:, 256:512].astype(qlat_ref.dtype)

    ckv = _rms(seg(_C_CKV, _C_GATES), kvn_ref[...])
    ckv_ref[...] = ckv
    kp_ref[:, 0:256] = ckv.astype(kp_ref.dtype)
    krt = seg(_C_KRT, _C_END)
    k1 = krt[:, 0:128]
    k2 = krt[:, 128:256]
    kt1 = k1 * cos - k2 * sin
    kt2 = k2 * cos + k1 * sin
    kp_ref[:, 256:384] = kt1.astype(kp_ref.dtype)
    kp_ref[:, 384:512] = kt2.astype(kp_ref.dtype)
    kr_ref[...] = jnp.concatenate([kt1[:, 0:16], kt2[:, 0:16]], axis=1)


def _inproj(x, lw, cos_tab, sin_tab):
    n = x.shape[0]
    tm = _tile(n, 512)
    ntab = cos_tab.shape[0] // tm
    row = lambda w: pl.BlockSpec((tm, w), lambda i: (i, 0))
    tab = pl.BlockSpec((tm, 128), lambda i: (i % ntab, 0))
    out_shape = (
        jax.ShapeDtypeStruct((n, 768), MXU_DTYPE),
        jax.ShapeDtypeStruct((n, 256), F32),
        jax.ShapeDtypeStruct((n, 256), F32),
        jax.ShapeDtypeStruct((n, 128), F32),
        jax.ShapeDtypeStruct((MLA_HEADS, n, 256), MXU_DTYPE),
        jax.ShapeDtypeStruct((n, 256), MXU_DTYPE),
        jax.ShapeDtypeStruct((n, 512), MXU_DTYPE),
        jax.ShapeDtypeStruct((n, 256), F32),
        jax.ShapeDtypeStruct((n, 32), F32),
    )
    out_specs = (row(768), row(256), row(256), row(128),
                 pl.BlockSpec((MLA_HEADS, tm, 256), lambda i: (0, i, 0)),
                 row(256), row(512), row(256), row(32))
    return pl.pallas_call(
        _inproj_body,
        out_shape=out_shape,
        grid=(n // tm,),
        in_specs=[row(D_MODEL), _const_spec((1, D_MODEL)), _const_spec((D_MODEL, _C_END)),
                  _const_spec((1, Q_LORA)), _const_spec((1, KV_LORA)), _const_spec((Q_LORA, 768)),
                  _const_spec((MLA_HEADS // 2, 128, 512)), tab, tab],
        out_specs=out_specs,
        compiler_params=_params(("parallel",)),
        name="inproj",
    )(x, lw["norm_mix"], lw["w_in"], lw["q_norm"], lw["kv_norm"], lw["w_uq"], lw["w_ukp"], cos_tab, sin_tab)


def _pool_body(pre_ref, u_ref, w_ref, sc_ref, y_ref, halo_ref, *, pos0, tc):
    t = pl.program_id(1)

    @pl.when(t == 0)
    def _():
        halo_ref[...] = pre_ref[...]

    u = u_ref[...]
    full = jnp.concatenate([halo_ref[...], u], axis=0)
    a2 = full + pltpu.roll(full, 1, 0)
    a4 = a2 + pltpu.roll(a2, 2, 0)
    a8 = a4 + pltpu.roll(a4, 4, 0)
    a16 = a8 + pltpu.roll(a8, 8, 0)
    lane = lax.broadcasted_iota(jnp.int32, (tc, D_POOL), 1)
    g0, g1, g2 = lane < 64, lane < 128, lane < 192
    win = jnp.where(g0, a2[POOL_HALO:], jnp.where(g1, a4[POOL_HALO:], jnp.where(g2, a8[POOL_HALO:], a16[POOL_HALO:])))
    wsz = jnp.where(g0, 2, jnp.where(g1, 4, jnp.where(g2, 8, 16)))
    pos = lax.broadcasted_iota(jnp.int32, (tc, D_POOL), 0) + (pos0 + t * tc)
    cnt = jnp.minimum(pos + 1, wsz).astype(F32)
    d = (win / cnt - u).astype(MXU_DTYPE)
    y_ref[...] = (_dot(d, w_ref[...]) * sc_ref[...]).astype(y_ref.dtype)
    halo_ref[...] = full[tc:, :]


def _pool(u, prefix, lw, pos0):
    b, t, _ = u.shape
    tc = _tile(t, 512)
    return pl.pallas_call(
        functools.partial(_pool_body, pos0=pos0, tc=tc),
        out_shape=jax.ShapeDtypeStruct((b, t, D_POOL), MXU_DTYPE),
        grid=(b, t // tc),
        in_specs=[pl.BlockSpec((None, POOL_HALO, D_POOL), lambda i, j: (i, 0, 0)),
                  pl.BlockSpec((None, tc, D_POOL), lambda i, j: (i, j, 0)),
                  _const_spec((D_POOL, D_POOL)), _const_spec((1, D_POOL))],
        out_specs=pl.BlockSpec((None, tc, D_POOL), lambda i, j: (i, j, 0)),
        scratch_shapes=[pltpu.VMEM((POOL_HALO, D_POOL), F32)],
        compiler_params=_params(("parallel", "arbitrary")),
        name="pool",
    )(prefix, u, lw["pool_w"], lw["pool_scale"])


def _head_lane_masks(shape, width):
    lane = lax.broadcasted_iota(jnp.int32, shape, len(shape) - 1)
    return [(lane >= width * h) & (lane < width * (h + 1)) for h in range(MLSTM_HEADS)]


def _by_head(masks, vals):
    out = jnp.where(masks[0], vals[0], 0.0)
    for h in range(1, MLSTM_HEADS):
        out = jnp.where(masks[h], vals[h], out)
    return out


def _mlstm_body(qkv_ref, opre_ref, gates_ref, gbias_ref, nrm_ref, c0_ref, n0_ref, m0_ref,
                y_ref, cout_ref, nout_ref, mout_ref, c_ref, n_ref, m_ref, *, L):
    t = pl.program_id(1)

    @pl.when(t == 0)
    def _():
        c_ref[...] = c0_ref[...]
        n_ref[...] = n0_ref[...]
        m_ref[...] = m0_ref[...]

    H = MLSTM_HEADS
    q = qkv_ref[:, 0:256]
    k = qkv_ref[:, 256:512]
    v = qkv_ref[:, 512:768]
    g = gates_ref[...] + gbias_ref[...]
    gt = g.T
    row_i = lax.broadcasted_iota(jnp.int32, (L, L), 0)
    col_i = lax.broadcasted_iota(jnp.int32, (L, L), 1)
    causal = col_i <= row_i
    tril = causal.astype(F32)
    b_cols = jnp.dot(tril, jax.nn.log_sigmoid(g), preferred_element_type=F32, precision=lax.Precision.HIGHEST)
    b_rows = jnp.dot(jax.nn.log_sigmoid(gt), (row_i <= col_i).astype(F32), preferred_element_type=F32,
                     precision=lax.Precision.HIGHEST)

    m_prev_all = m_ref[...]
    masks = _head_lane_masks((L, D_MLSTM), MLSTM_DH)
    qf = q.astype(F32)
    kf = k.astype(F32)
    zero_q = jnp.zeros_like(q)
    qn = qf * n_ref[...]

    num_parts, den_col, mt_col, a_col, wk_col, dec_11, mnew_11 = [], [], [], [], [], [], []
    for h in range(H):
        b_col = b_cols[:, 4 + h:5 + h]
        b_row = b_rows[4 + h:5 + h, :]
        ig_row = gt[h:h + 1, :]
        ig_col = g[:, h:h + 1]
        m_prev = m_prev_all[:, h:h + 1]
        logw = jnp.where(causal, b_col - b_row + ig_row, -jnp.inf)
        inter = b_col + m_prev
        m_t = jnp.maximum(inter, jnp.max(logw, axis=1, keepdims=True))
        w = jnp.exp(logw - m_t)
        a = jnp.exp(inter - m_t)
        qh = jnp.where(masks[h], q, zero_q)
        s = _dot_nt(qh, k) * w
        num_parts.append(_dot(s.astype(MXU_DTYPE), v))
        den_col.append(jnp.sum(s, axis=1, keepdims=True) + a * jnp.sum(jnp.where(masks[h], qn, 0.0), axis=1, keepdims=True))
        mt_col.append(m_t)
        a_col.append(a)
        m_new = m_t[L - 1:L, :]
        b_last = b_col[L - 1:L, :]
        mnew_11.append(m_new)
        dec_11.append(jnp.exp(b_last + m_prev - m_new))
        wk_col.append(jnp.exp(b_last - b_col + ig_col - m_new))

    inter_num = _dot(q, c_ref[...].astype(MXU_DTYPE))
    num = _by_head(masks, num_parts) + _by_head(masks, a_col) * inter_num
    denom = _by_head(masks, [jnp.maximum(jnp.abs(den_col[h]), jnp.exp(-mt_col[h])) for h in range(H)])
    hm = num / denom
    hsq = hm * hm
    ms = _by_head(masks, [jnp.sum(jnp.where(masks[h], hsq, 0.0), axis=1, keepdims=True) * (1.0 / MLSTM_DH) for h in range(H)])
    hn = hm * lax.rsqrt(ms + EPS) * nrm_ref[...]
    y_ref[...] = (hn * jax.nn.sigmoid(opre_ref[...])).astype(y_ref.dtype)

    kw = kf * _by_head(masks, wk_col)
    c_add = _dot(kw.T.astype(MXU_DTYPE), v)
    masks_sq = _head_lane_masks((D_MLSTM, D_MLSTM), MLSTM_DH)
    rows_sq = lax.broadcasted_iota(jnp.int32, (D_MLSTM, D_MLSTM), 0)
    diag = jnp.zeros((D_MLSTM, D_MLSTM), jnp.bool_)
    for h in range(H):
        diag = diag | (masks_sq[h] & (rows_sq >= MLSTM_DH * h) & (rows_sq < MLSTM_DH * (h + 1)))
    lane1 = _head_lane_masks((1, D_MLSTM), MLSTM_DH)
    dec_lane = _by_head(lane1, dec_11)
    c_new = dec_lane * c_ref[...] + jnp.where(diag, c_add, 0.0)
    n_new = dec_lane * n_ref[...] + jnp.sum(kw, axis=0, keepdims=True)
    lane_m = lax.broadcasted_iota(jnp.int32, (1, 128), 1)
    m_new_all = m_prev_all
    for h in range(H):
        m_new_all = jnp.where(lane_m == h, mnew_11[h], m_new_all)
    c_ref[...] = c_new
    n_ref[...] = n_new
    m_ref[...] = m_new_all

    @pl.when(t == pl.num_programs(1) - 1)
    def _():
        cout_ref[...] = c_new
        nout_ref[...] = n_new
        mout_ref[...] = m_new_all


def _mlstm(qkv, opre, gates, lw, c0, n0, m0):
    b, t, _ = qkv.shape
    L = MLSTM_CHUNK
    seq = lambda w: pl.BlockSpec((None, L, w), lambda i, j: (i, j, 0))
    st = lambda r, w: pl.BlockSpec((None, r, w), lambda i, j: (i, 0, 0))
    return pl.pallas_call(
        functools.partial(_mlstm_body, L=L),
        out_shape=(jax.ShapeDtypeStruct((b, t, D_MLSTM), MXU_DTYPE),
                   jax.ShapeDtypeStruct((b, D_MLSTM, D_MLSTM), F32),
                   jax.ShapeDtypeStruct((b, 1, D_MLSTM), F32),
                   jax.ShapeDtypeStruct((b, 1, 128), F32)),
        grid=(b, t // L),
        in_specs=[seq(768), seq(256), seq(128), _const_spec((1, 128)), _const_spec((1, D_MLSTM)),
                  st(D_MLSTM, D_MLSTM), st(1, D_MLSTM), st(1, 128)],
        out_specs=(seq(256), st(D_MLSTM, D_MLSTM), st(1, D_MLSTM), st(1, 128)),
        scratch_shapes=[pltpu.VMEM((D_MLSTM, D_MLSTM), F32), pltpu.VMEM((1, D_MLSTM), F32), pltpu.VMEM((1, 128), F32)],
        compiler_params=_params(("parallel", "arbitrary")),
        name="mlstm",
    )(qkv, opre, gates, lw["gbias"], lw["mlstm_norm"], c0, n0, m0)


def _head_proj(o_heads, wuvp_ref, y_ref):
    for p in range(MLA_HEADS // 2):
        op = jnp.concatenate([o_heads(2 * p), o_heads(2 * p + 1)], axis=1).astype(MXU_DTYPE)
        y_ref[:, 128 * p:128 * p + 128] = _dot(op, wuvp_ref[p]).astype(y_ref.dtype)


def _attn_body(qlat_ref, r_ref, k_ref, wuvp_ref, y_ref, q2_ref, m_ref, l_ref, acc_ref, *, tq):
    i = pl.program_id(1)
    rows = MLA_HEADS * tq
    r = r_ref[...]
    lane = lax.broadcasted_iota(jnp.int32, (tq, 256), 1) & 127
    zero_r = jnp.zeros_like(r)
    for h in range(MLA_HEADS):
        q2_ref[h * tq:(h + 1) * tq, 0:256] = qlat_ref[h]
        q2_ref[h * tq:(h + 1) * tq, 256:512] = jnp.where((lane >= 16 * h) & (lane < 16 * h + 16), r, zero_r)
    m_ref[...] = jnp.full((rows, 1), -jnp.inf, F32)
    l_ref[...] = jnp.zeros((rows, 1), F32)
    acc_ref[...] = jnp.zeros((rows, 256), F32)

    def block(j, masked):
        kb = k_ref[pl.ds(pl.multiple_of(j * tq, tq), tq), :]
        s = _dot_nt(q2_ref[...], kb) * ATTN_SCALE
        if masked:
            qpos = lax.broadcasted_iota(jnp.int32, (rows, tq), 0) & (tq - 1)
            kpos = lax.broadcasted_iota(jnp.int32, (rows, tq), 1)
            s = jnp.where(kpos <= qpos, s, -jnp.inf)
        m_old = m_ref[...]
        m_new = jnp.maximum(m_old, jnp.max(s, axis=1, keepdims=True))
        alpha = jnp.exp(m_old - m_new)
        p = jnp.exp(s - m_new)
        l_ref[...] = alpha * l_ref[...] + jnp.sum(p, axis=1, keepdims=True)
        acc_ref[...] = alpha * acc_ref[...] + _dot(p.astype(MXU_DTYPE), kb[:, 0:256])
        m_ref[...] = m_new

    def body(j, carry):
        block(j, False)
        return carry

    lax.fori_loop(0, i, body, 0)
    block(i, True)

    def o_head(h):
        return acc_ref[h * tq:(h + 1) * tq, :] / l_ref[h * tq:(h + 1) * tq, :]

    _head_proj(o_head, wuvp_ref, y_ref)


def _attn_prompt(qlat, r, kp, lw, b, t):
    tq = _tile(t, 256)
    nq = t // tq
    rows = MLA_HEADS * tq
    return pl.pallas_call(
        functools.partial(_attn_body, tq=tq),
        out_shape=jax.ShapeDtypeStruct((b * t, D_MLA), MXU_DTYPE),
        grid=(b, nq),
        in_specs=[pl.BlockSpec((MLA_HEADS, tq, 256), lambda i, j: (0, i * nq + j, 0)),
                  pl.BlockSpec((tq, 256), lambda i, j: (i * nq + j, 0)),
                  pl.BlockSpec((t, 512), lambda i, j: (i, 0)),
                  _const_spec((MLA_HEADS // 2, 512, 128))],
        out_specs=pl.BlockSpec((tq, D_MLA), lambda i, j: (i * nq + j, 0)),
        scratch_shapes=[pltpu.VMEM((rows, 512), MXU_DTYPE), pltpu.VMEM((rows, 1), F32),
                        pltpu.VMEM((rows, 1), F32), pltpu.VMEM((rows, 256), F32)],
        compiler_params=_params(("parallel", "arbitrary")),
        name="attn_prompt",
    )(qlat, r, kp, lw["w_uvp"])


NEW_PAD = 16


def _attn_sample_body(pt_ref, ql_ref, qr_ref, cn_ref, kn_ref, *rest, G, T):
    lat_refs = rest[:G]
    kr_refs = rest[G:2 * G]
    o_ref, m_ref, l_ref, acc_ref = rest[2 * G:]
    j = pl.program_id(1)
    rows = MLA_HEADS * T

    @pl.when(j == 0)
    def _():
        m_ref[...] = jnp.full((rows, 1), -jnp.inf, F32)
        l_ref[...] = jnp.zeros((rows, 1), F32)
        acc_ref[...] = jnp.zeros((rows, KV_LORA), F32)

    ql = ql_ref[...]
    qr = qr_ref[...]

    def update(s, vals):
        m_old = m_ref[...]
        m_new = jnp.maximum(m_old, jnp.max(s, axis=1, keepdims=True))
        alpha = jnp.exp(m_old - m_new)
        pf = jnp.exp(s - m_new)
        p = pf.astype(MXU_DTYPE)
        l_ref[...] = alpha * l_ref[...] + jnp.sum(pf, axis=1, keepdims=True)
        acc = alpha * acc_ref[...]
        off = 0
        for vb in vals:
            n = vb.shape[0]
            acc = acc + _dot(p[:, off:off + n], vb)
            off += n
        acc_ref[...] = acc
        m_ref[...] = m_new

    cls = [lat_refs[g][...].astype(MXU_DTYPE) for g in range(G)]
    s = jnp.concatenate([_dot_nt(ql, cls[g]) + _dot_nt(qr, kr_refs[g][...].astype(MXU_DTYPE)) for g in range(G)], axis=1)
    update(s * ATTN_SCALE, cls)

    @pl.when(j == pl.num_programs(1) - 1)
    def _():
        cn = cn_ref[...].astype(MXU_DTYPE)
        kn = kn_ref[...].astype(MXU_DTYPE)
        sn = (_dot_nt(ql, cn) + _dot_nt(qr, kn)) * ATTN_SCALE
        qpos = lax.broadcasted_iota(jnp.int32, sn.shape, 0) % T
        kpos = lax.broadcasted_iota(jnp.int32, sn.shape, 1)
        update(jnp.where(kpos <= qpos, sn, -jnp.inf), [cn])
        o_ref[...] = acc_ref[...] / l_ref[...]


def _attn_sample(ql, qr, cn, kn, cache_latent, cache_krope, page_table, layer, G):
    db, rows, _ = ql.shape
    T = rows // MLA_HEADS
    n_pages = page_table.shape[1]
    page = cache_latent.shape[2]
    G = _tile(n_pages, G)
    per_b = lambda r, w: pl.BlockSpec((None, r, w), lambda i, j, pt: (i, 0, 0))
    lat_specs = [pl.BlockSpec((None, None, page, KV_LORA),
                              functools.partial(lambda i, j, pt, g: (layer, pt[i * n_pages + j * G + g], 0, 0), g=g))
                 for g in range(G)]
    kr_specs = [pl.BlockSpec((None, None, page, QK_ROPE),
                             functools.partial(lambda i, j, pt, g: (layer, pt[i * n_pages + j * G + g], 0, 0), g=g))
                for g in range(G)]
    grid_spec = pltpu.PrefetchScalarGridSpec(
        num_scalar_prefetch=1,
        grid=(db, n_pages // G),
        in_specs=[per_b(rows, KV_LORA), per_b(rows, QK_ROPE), per_b(NEW_PAD, KV_LORA), per_b(NEW_PAD, QK_ROPE)] + lat_specs + kr_specs,
        out_specs=pl.BlockSpec((None, rows, KV_LORA), lambda i, j, pt: (i, 0, 0)),
        scratch_shapes=[pltpu.VMEM((rows, 1), F32), pltpu.VMEM((rows, 1), F32), pltpu.VMEM((rows, KV_LORA), F32)],
    )
    return pl.pallas_call(
        functools.partial(_attn_sample_body, G=G, T=T),
        out_shape=jax.ShapeDtypeStruct((db, rows, KV_LORA), F32),
        grid_spec=grid_spec,
        compiler_params=_params(("parallel", "arbitrary")),
        name="attn_sample",
    )(page_table.reshape(-1), ql, qr, cn, kn, *([cache_latent] * G), *([cache_krope] * G))


def _uvproj_body(o_ref, wuvp_ref, y_ref):
    _head_proj(lambda h: o_ref[h], wuvp_ref, y_ref)


def _uvproj(o, lw):
    n = o.shape[1]
    return pl.pallas_call(
        _uvproj_body,
        out_shape=jax.ShapeDtypeStruct((n, D_MLA), MXU_DTYPE),
        grid=(1,),
        in_specs=[pl.BlockSpec((MLA_HEADS, n, 256), lambda i: (0, 0, 0)), _const_spec((MLA_HEADS // 2, 512, 128))],
        out_specs=pl.BlockSpec((n, D_MLA), lambda i: (0, 0)),
        compiler_params=_params(("arbitrary",)),
        name="uvproj",
    )(o, lw["w_uvp"])


FFN_CHUNK = 256


def _ffn_body(x_ref, yp_ref, ym_ref, ya_ref, p_ref, wout_ref, nf_ref, wg_ref, wu_ref, wd_ref, pn_ref, wpg_ref,
              wpp_ref, fn_ref, xo_ref, *maybe_yo, final):
    mix = (_dot(yp_ref[...], wout_ref[0:256, :]) + _dot(ym_ref[...], wout_ref[256:512, :])
           + _dot(ya_ref[...], wout_ref[512:1024, :]))
    x1 = x_ref[...] + mix
    hf = _rms(x1, nf_ref[...]).astype(MXU_DTYPE)
    acc = jnp.zeros_like(x1)
    for c in range(D_FF // FFN_CHUNK):
        sl = slice(c * FFN_CHUNK, (c + 1) * FFN_CHUNK)
        gate = _dot(hf, wg_ref[:, sl])
        up = _dot(hf, wu_ref[:, sl])
        acc = acc + _dot((gate * jax.nn.sigmoid(gate) * up).astype(MXU_DTYPE), wd_ref[sl, :])
    x2 = x1 + acc
    pg = jax.nn.sigmoid(_dot(_rms(x2, pn_ref[...]).astype(MXU_DTYPE), wpg_ref[...]))
    x3 = x2 + pg * _dot(p_ref[...].astype(MXU_DTYPE), wpp_ref[...])
    xo_ref[...] = x3
    if final:
        maybe_yo[0][...] = _rms(x3, fn_ref[...])


def _ffn(x, yp, ym, ya, p, lw, final_norm, final):
    n = x.shape[0]
    tm = _tile(n, 512)
    row = lambda w: pl.BlockSpec((tm, w), lambda i: (i, 0))
    out_shape = [jax.ShapeDtypeStruct((n, D_MODEL), F32)]
    out_specs = [row(D_MODEL)]
    if final:
        out_shape.append(jax.ShapeDtypeStruct((n, D_MODEL), F32))
        out_specs.append(row(D_MODEL))
    return pl.pallas_call(
        functools.partial(_ffn_body, final=final),
        out_shape=tuple(out_shape),
        grid=(n // tm,),
        in_specs=[row(D_MODEL), row(D_POOL), row(D_MLSTM), row(D_MLA), row(D_PLE),
                  _const_spec((D_MODEL, D_MODEL)), _const_spec((1, D_MODEL)),
                  _const_spec((D_MODEL, D_FF)), _const_spec((D_MODEL, D_FF)), _const_spec((D_FF, D_MODEL)),
                  _const_spec((1, D_MODEL)), _const_spec((D_MODEL, D_MODEL)), _const_spec((D_PLE, D_MODEL)),
                  _const_spec((1, D_MODEL))],
        out_specs=tuple(out_specs),
        compiler_params=_params(("parallel",)),
        name="ffn",
    )(x, yp, ym, ya, p, lw["w_out"], lw["norm_ffn"], lw["w_gate"], lw["w_up"], lw["w_down"], lw["ple_norm"],
      lw["w_ple_gate"], lw["w_ple_proj"], final_norm)


def _block_diag(blocks):
    n = len(blocks)
    r, c = blocks[0].shape
    out = jnp.zeros((n * r, n * c), blocks[0].dtype)
    for i, blk in enumerate(blocks):
        out = out.at[i * r:(i + 1) * r, i * c:(i + 1) * c].set(blk)
    return out


def _pack_layer(i, norm_mix, w_in, b_igate, b_fgate, pool_w, pool_scale, mlstm_norm, q_norm, w_uq, kv_norm, w_uk,
                w_uv, w_out, norm_ffn, w_gate, w_up, w_down, ple_norm, w_ple_gate, w_ple_proj):
    wi = w_in[i]
    kr = wi[:, 1928:1960]
    packed = jnp.concatenate([
        wi[:, 256:1024], wi[:, 1024:1280], wi[:, 0:256], wi[:, 1288:1672], wi[:, 1672:1928],
        wi[:, 1280:1288], jnp.zeros((D_MODEL, 120), wi.dtype),
        jnp.tile(kr[:, 0:16], (1, MLA_HEADS)), jnp.tile(kr[:, 16:32], (1, MLA_HEADS))], axis=1)
    wq = w_uq[i]
    w_uq_p = jnp.concatenate([wq[:, :, 0:64].reshape(Q_LORA, 512), wq[:, :, 64:80].reshape(Q_LORA, 128),
                              wq[:, :, 80:96].reshape(Q_LORA, 128)], axis=1)
    wk = w_uk[i]
    w_ukp = jnp.stack([_block_diag([wk[:, 2 * p, :].T, wk[:, 2 * p + 1, :].T]) for p in range(MLA_HEADS // 2)])
    wv = w_uv[i]
    w_uvp = jnp.stack([_block_diag([wv[:, 2 * p, :], wv[:, 2 * p + 1, :]]) for p in range(MLA_HEADS // 2)])
    gbias = jnp.concatenate([b_igate[i], b_fgate[i], jnp.zeros((120,), F32)]).reshape(1, 128)
    mx = lambda a: a.astype(MXU_DTYPE)
    return dict(
        norm_mix=norm_mix[i].reshape(1, -1), w_in=mx(packed), q_norm=q_norm[i].reshape(1, -1),
        kv_norm=kv_norm[i].reshape(1, -1), w_uq=mx(w_uq_p), w_ukp=mx(w_ukp), w_uvp=mx(w_uvp),
        pool_w=mx(_block_diag([pool_w[i, g] for g in range(4)])), pool_scale=pool_scale[i].reshape(1, -1),
        gbias=gbias, mlstm_norm=mlstm_norm[i].reshape(1, -1),
        w_out=mx(w_out[i]), norm_ffn=norm_ffn[i].reshape(1, -1), w_gate=mx(w_gate[i]), w_up=mx(w_up[i]),
        w_down=mx(w_down[i]), ple_norm=ple_norm[i].reshape(1, -1), w_ple_gate=mx(w_ple_gate[i]),
        w_ple_proj=mx(w_ple_proj[i]))


def _rope_tables(pos):
    inv = jnp.power(ROPE_THETA, -jnp.arange(0, QK_ROPE, 2, dtype=F32) / QK_ROPE)
    ang = pos.astype(F32)[:, None] * inv[None, :]
    return jnp.tile(jnp.cos(ang), (1, MLA_HEADS)), jnp.tile(jnp.sin(ang), (1, MLA_HEADS))


def _diag_blocks(cbd):
    return jnp.stack([cbd[:, 64 * h:64 * h + 64, 64 * h:64 * h + 64] for h in range(MLSTM_HEADS)], axis=1)


def _pad_axis1(a, n, value=0.0):
    pad = [(0, 0)] * a.ndim
    pad[1] = (0, n - a.shape[1])
    return jnp.pad(a, pad, constant_values=value)


def _layer(x, p, lw, cos_tab, sin_tab, b, t, pos0, prefix, c0, n0, m0, attend, final_norm, final):
    qkv, opre, upool, gates, qlat, r, kp, ckv, kr = _inproj(x, lw, cos_tab, sin_tab)
    upool3 = upool.reshape(b, t, D_POOL)
    y_pool = _pool(_pad_axis1(upool3, -(-t // 8) * 8), prefix, lw, pos0)[:, :t].reshape(b * t, D_POOL)
    pool_state = jnp.concatenate([prefix[:, 1:], upool3], axis=1)[:, -POOL_STATE:]

    L = MLSTM_CHUNK
    tp = -(-t // L) * L
    qkv3, opre3, gates3 = qkv.reshape(b, t, 768), opre.reshape(b, t, 256), gates.reshape(b, t, 128)
    if tp != t:
        lane = jnp.arange(128)
        padrow = jnp.where(lane < 4, NEG_BIG, jnp.where(lane < 8, 1e4, 0.0)).astype(F32)
        gates3 = jnp.concatenate([gates3, jnp.broadcast_to(padrow, (b, tp - t, 128))], axis=1)
        qkv3, opre3 = _pad_axis1(qkv3, tp), _pad_axis1(opre3, tp)
    y_ml, c_bd, n_new, m_new = _mlstm(qkv3, opre3, gates3, lw, c0, n0, m0)
    y_ml = y_ml[:, :t].reshape(b * t, D_MLSTM)

    y_mla = attend(qlat, r, kp, ckv, kr)
    outs = _ffn(x, y_pool, y_ml, y_mla, p, lw, final_norm, final)
    states = (ckv.reshape(b, t, KV_LORA), kr.reshape(b, t, QK_ROPE), pool_state, _diag_blocks(c_bd),
              n_new.reshape(b, MLSTM_HEADS, MLSTM_DH), m_new[:, 0, :MLSTM_HEADS])
    return outs, states


SAMPLE_PAGES_PER_STEP = 16


def kernel(x_prompt, x_sample, cache_latent, cache_krope, state_pool, state_C, state_n, state_m, page_table, p_prompt, p_sample, norm_mix, w_in, b_igate, b_fgate, pool_w, pool_scale, mlstm_norm, q_norm, w_uq, kv_norm, w_uk, w_uv, w_out, norm_ffn, w_gate, w_up, w_down, ple_norm, w_ple_gate, w_ple_proj, final_norm):
    bp, tp_, _ = x_prompt.shape
    bs, ts, _ = x_sample.shape
    depth = w_in.shape[0]
    past_len = page_table.shape[1] * cache_latent.shape[2]
    cos_p, sin_p = _rope_tables(jnp.arange(tp_))
    cos_s, sin_s = _rope_tables(past_len + jnp.arange(ts))
    cos_s, sin_s = jnp.tile(cos_s, (bs, 1)), jnp.tile(sin_s, (bs, 1))
    fin = final_norm.reshape(1, -1)

    xp = x_prompt.reshape(bp * tp_, D_MODEL)
    xs = x_sample.reshape(bs * ts, D_MODEL)
    st_p, st_s = [], []
    yp = ys = None
    for i in range(depth):
        lw = _pack_layer(i, norm_mix, w_in, b_igate, b_fgate, pool_w, pool_scale, mlstm_norm, q_norm, w_uq, kv_norm,
                         w_uk, w_uv, w_out, norm_ffn, w_gate, w_up, w_down, ple_norm, w_ple_gate, w_ple_proj)
        final = i == depth - 1

        def attend_p(qlat, r, kp, ckv, kr):
            return _attn_prompt(qlat, r, kp, lw, bp, tp_)

        outs, sp = _layer(xp, p_prompt[i].reshape(bp * tp_, D_PLE), lw, cos_p, sin_p, bp, tp_, 0,
                          jnp.zeros((bp, POOL_HALO, D_POOL), F32), jnp.zeros((bp, D_MLSTM, D_MLSTM), F32),
                          jnp.zeros((bp, 1, D_MLSTM), F32), jnp.zeros((bp, 1, 128), F32), attend_p, fin, final)
        xp = outs[0]
        if final:
            yp = outs[1]

        def attend_s(qlat, r, kp, ckv, kr):
            ql = qlat.reshape(MLA_HEADS, bs, ts, KV_LORA).transpose(1, 0, 2, 3).reshape(bs, MLA_HEADS * ts, KV_LORA)
            qr = r.reshape(bs, ts, 2, MLA_HEADS, 16).transpose(0, 3, 1, 2, 4).reshape(bs, MLA_HEADS * ts, QK_ROPE)
            cn = _pad_axis1(ckv.reshape(bs, ts, KV_LORA), NEW_PAD)
            kn = _pad_axis1(kr.reshape(bs, ts, QK_ROPE), NEW_PAD)
            o = _attn_sample(ql, qr, cn, kn, cache_latent, cache_krope, page_table, i, SAMPLE_PAGES_PER_STEP)
            o = o.reshape(bs, MLA_HEADS, ts, KV_LORA).transpose(1, 0, 2, 3).reshape(MLA_HEADS, bs * ts, KV_LORA)
            return _uvproj(o, lw)

        prefix_s = jnp.concatenate([jnp.zeros((bs, 1, D_POOL), F32), state_pool[i]], axis=1)
        c0 = jnp.zeros((bs, D_MLSTM, D_MLSTM), F32)
        for h in range(MLSTM_HEADS):
            c0 = c0.at[:, 64 * h:64 * h + 64, 64 * h:64 * h + 64].set(state_C[i][:, h])
        n0 = state_n[i].reshape(bs, 1, D_MLSTM)
        m0 = _pad_axis1(state_m[i], 128)[:, None, :]
        outs, ss = _layer(xs, p_sample[i].reshape(bs * ts, D_PLE), lw, cos_s, sin_s, bs, ts, past_len,
                          prefix_s, c0, n0, m0, attend_s, fin, final)
        xs = outs[0]
        if final:
            ys = outs[1]
        st_p.append(sp)
        st_s.append(ss)

    stack = lambda sts, k: jnp.stack([s[k] for s in sts])
    return (yp.reshape(bp, tp_, D_MODEL), ys.reshape(bs, ts, D_MODEL),
            stack(st_p, 0), stack(st_p, 1), stack(st_p, 2), stack(st_p, 3), stack(st_p, 4), stack(st_p, 5),
            stack(st_s, 0), stack(st_s, 1), stack(st_s, 2), stack(st_s, 3), stack(st_s, 4), stack(st_s, 5))
```

```python
import functools
import math

import numpy as np
import jax
import jax.numpy as jnp
from jax import lax
from jax.experimental import pallas as pl
from jax.experimental.pallas import tpu as pltpu

D_MODEL = 1024
POOL_WINDOWS = (2, 4, 8, 16)
POOL_GROUP = 64
D_POOL = 256
POOL_STATE = 15
MLSTM_HEADS = 4
MLSTM_DH = 64
D_MLSTM = 256
MLA_HEADS = 8
QK_NOPE = 64
QK_ROPE = 32
V_DIM = 64
D_MLA = 512
Q_LORA = 384
KV_LORA = 256
ROPE_THETA = 10000.0
ATTN_SCALE = (QK_NOPE + QK_ROPE) ** -0.5
SCORE_SCALE_LOG2 = ATTN_SCALE * math.log2(math.e)
D_FF = 2816
D_PLE = 256
EPS = 1e-6

F32 = jnp.float32
MXU_DTYPE = jnp.bfloat16
POOL_HALO = 16
MLSTM_CHUNK = 128
NEG_BIG = -1e30
VMEM_LIMIT = 56 * 1024 * 1024

_C_QKV = 0
_C_OPRE = 768
_C_UPOOL = 1024
_C_CQ = 1280
_C_CKV = 1664
_C_GATES = 1920
_C_KRT = 2048
_C_END = 2304


def _dot(a, b):
    return jnp.dot(a, b, preferred_element_type=F32)


def _dot_nt(a, b):
    return lax.dot_general(a, b, (((1,), (1,)), ((), ())), preferred_element_type=F32)


def _rms(x, g):
    return x * lax.rsqrt(jnp.mean(x * x, axis=-1, keepdims=True) + EPS) * g


def _tile(n, pref):
    t = min(n, pref)
    while n % t:
        t //= 2
    return t


def _const_spec(shape):
    nd = len(shape)
    return pl.BlockSpec(shape, lambda *_: (0,) * nd, pipeline_mode=pl.Buffered(1))


def _params(sem):
    return pltpu.CompilerParams(dimension_semantics=sem, vmem_limit_bytes=VMEM_LIMIT)


def _inproj_body(x_ref, nrm_ref, w_ref, qn_ref, kvn_ref, wuq_ref, wukp_ref, cos_ref, sin_ref,
                 qkv_ref, opre_ref, upool_ref, gates_ref, qlat_ref, r_ref, kp_ref, ckv_ref, kr_ref, kt_ref):
    hb = _rms(x_ref[...], nrm_ref[...]).astype(MXU_DTYPE)

    def seg(a, b):
        return _dot(hb, w_ref[:, a:b])

    qkv_ref[:, 0:256] = seg(_C_QKV, _C_QKV + 256).astype(qkv_ref.dtype)
    qkv_ref[:, 256:512] = (seg(_C_QKV + 256, _C_QKV + 512) * (MLSTM_DH ** -0.5)).astype(qkv_ref.dtype)
    qkv_ref[:, 512:768] = seg(_C_QKV + 512, _C_QKV + 768).astype(qkv_ref.dtype)
    opre_ref[...] = seg(_C_OPRE, _C_UPOOL)
    upool_ref[...] = seg(_C_UPOOL, _C_CQ)
    gates_ref[...] = seg(_C_GATES, _C_KRT)

    cos = cos_ref[...]
    sin = sin_ref[...]

    cqn = _rms(seg(_C_CQ, _C_CKV), qn_ref[...]).astype(MXU_DTYPE)
    qf = _dot(cqn, wuq_ref[...])
    r1 = qf[:, 512:640]
    r2 = qf[:, 640:768]
    r_ref[:, 0:128] = (r1 * cos - r2 * sin).astype(r_ref.dtype)
    r_ref[:, 128:256] = (r2 * cos + r1 * sin).astype(r_ref.dtype)
    for p in range(MLA_HEADS // 2):
        ql = _dot(qf[:, 128 * p:128 * p + 128].astype(MXU_DTYPE), wukp_ref[p])
        qlat_ref[2 * p] = ql[:, 0:256].astype(qlat_ref.dtype)
        qlat_ref[2 * p + 1] = ql[:, 256:512].astype(qlat_ref.dtype)

    ckv = _rms(seg(_C_CKV, _C_GATES), kvn_ref[...])
    ckv_ref[...] = ckv
    kp_ref[:, 0:256] = ckv.astype(kp_ref.dtype)
    for c in range(kt_ref.shape[0]):
        kt_ref[c] = ckv[c * kt_ref.shape[2]:(c + 1) * kt_ref.shape[2], :].T.astype(kt_ref.dtype)
    krt = seg(_C_KRT, _C_END)
    k1 = krt[:, 0:128]
    k2 = krt[:, 128:256]
    kt1 = k1 * cos - k2 * sin
    kt2 = k2 * cos + k1 * sin
    kp_ref[:, 256:384] = kt1.astype(kp_ref.dtype)
    kp_ref[:, 384:512] = kt2.astype(kp_ref.dtype)
    kr_ref[...] = jnp.concatenate([kt1[:, 0:16], kt2[:, 0:16]], axis=1)


def _inproj(x, lw, cos_tab, sin_tab):
    n = x.shape[0]
    tm = _tile(n, 512)
    tkb = min(ATTN_TQ, tm)
    ntab = cos_tab.shape[0] // tm
    row = lambda w: pl.BlockSpec((tm, w), lambda i: (i, 0))
    tab = pl.BlockSpec((tm, 128), lambda i: (i % ntab, 0))
    out_shape = (
        jax.ShapeDtypeStruct((n, 768), MXU_DTYPE),
        jax.ShapeDtypeStruct((n, 256), F32),
        jax.ShapeDtypeStruct((n, 256), F32),
        jax.ShapeDtypeStruct((n, 128), F32),
        jax.ShapeDtypeStruct((MLA_HEADS, n, 256), MXU_DTYPE),
        jax.ShapeDtypeStruct((n, 256), MXU_DTYPE),
        jax.ShapeDtypeStruct((n, 512), MXU_DTYPE),
        jax.ShapeDtypeStruct((n, 256), F32),
        jax.ShapeDtypeStruct((n, 32), F32),
        jax.ShapeDtypeStruct((n // tkb, KV_LORA, tkb), MXU_DTYPE),
    )
    out_specs = (row(768), row(256), row(256), row(128),
                 pl.BlockSpec((MLA_HEADS, tm, 256), lambda i: (0, i, 0)),
                 row(256), row(512), row(256), row(32),
                 pl.BlockSpec((tm // tkb, KV_LORA, tkb), lambda i: (i, 0, 0)))
    return pl.pallas_call(
        _inproj_body,
        out_shape=out_shape,
        grid=(n // tm,),
        in_specs=[row(D_MODEL), _const_spec((1, D_MODEL)), _const_spec((D_MODEL, _C_END)),
                  _const_spec((1, Q_LORA)), _const_spec((1, KV_LORA)), _const_spec((Q_LORA, 768)),
                  _const_spec((MLA_HEADS // 2, 128, 512)), tab, tab],
        out_specs=out_specs,
        compiler_params=_params(("parallel",)),
        name="inproj",
    )(x, lw["norm_mix"], lw["w_in"], lw["q_norm"], lw["kv_norm"], lw["w_uq"], lw["w_ukp"], cos_tab, sin_tab)


def _pool_body(pre_ref, u_ref, w_ref, sc_ref, y_ref, halo_ref, *, pos0, tc):
    t = pl.program_id(1)

    @pl.when(t == 0)
    def _():
        halo_ref[...] = pre_ref[...]

    u = u_ref[...]
    full = jnp.concatenate([halo_ref[...], u], axis=0)
    a2 = full + pltpu.roll(full, 1, 0)
    a4 = a2 + pltpu.roll(a2, 2, 0)
    a8 = a4 + pltpu.roll(a4, 4, 0)
    a16 = a8 + pltpu.roll(a8, 8, 0)
    lane = lax.broadcasted_iota(jnp.int32, (tc, D_POOL), 1)
    g0, g1, g2 = lane < 64, lane < 128, lane < 192
    win = jnp.where(g0, a2[POOL_HALO:], jnp.where(g1, a4[POOL_HALO:], jnp.where(g2, a8[POOL_HALO:], a16[POOL_HALO:])))
    wsz = jnp.where(g0, 2, jnp.where(g1, 4, jnp.where(g2, 8, 16)))
    pos = lax.broadcasted_iota(jnp.int32, (tc, D_POOL), 0) + (pos0 + t * tc)
    cnt = jnp.minimum(pos + 1, wsz).astype(F32)
    d = (win / cnt - u).astype(MXU_DTYPE)
    y_ref[...] = (_dot(d, w_ref[...]) * sc_ref[...]).astype(y_ref.dtype)
    halo_ref[...] = full[tc:, :]


def _pool(u, prefix, lw, pos0):
    b, t, _ = u.shape
    tc = _tile(t, 512)
    return pl.pallas_call(
        functools.partial(_pool_body, pos0=pos0, tc=tc),
        out_shape=jax.ShapeDtypeStruct((b, t, D_POOL), MXU_DTYPE),
        grid=(b, t // tc),
        in_specs=[pl.BlockSpec((None, POOL_HALO, D_POOL), lambda i, j: (i, 0, 0)),
                  pl.BlockSpec((None, tc, D_POOL), lambda i, j: (i, j, 0)),
                  _const_spec((D_POOL, D_POOL)), _const_spec((1, D_POOL))],
        out_specs=pl.BlockSpec((None, tc, D_POOL), lambda i, j: (i, j, 0)),
        scratch_shapes=[pltpu.VMEM((POOL_HALO, D_POOL), F32)],
        compiler_params=_params(("parallel", "arbitrary")),
        name="pool",
    )(prefix, u, lw["pool_w"], lw["pool_scale"])


def _head_lane_masks(shape, width):
    lane = lax.broadcasted_iota(jnp.int32, shape, len(shape) - 1)
    return [(lane >= width * h) & (lane < width * (h + 1)) for h in range(MLSTM_HEADS)]


def _by_head(masks, vals):
    out = jnp.where(masks[0], vals[0], 0.0)
    for h in range(1, MLSTM_HEADS):
        out = jnp.where(masks[h], vals[h], out)
    return out


def _mlstm_body(qkv_ref, opre_ref, gates_ref, gbias_ref, nrm_ref, c0_ref, n0_ref, m0_ref,
                y_ref, cout_ref, nout_ref, mout_ref, c_ref, n_ref, m_ref, *, L):
    t = pl.program_id(1)

    @pl.when(t == 0)
    def _():
        c_ref[...] = c0_ref[...]
        n_ref[...] = n0_ref[...]
        m_ref[...] = m0_ref[...]

    H = MLSTM_HEADS
    q = qkv_ref[:, 0:256]
    k = qkv_ref[:, 256:512]
    v = qkv_ref[:, 512:768]
    g = gates_ref[...] + gbias_ref[...]
    gt = g.T
    row_i = lax.broadcasted_iota(jnp.int32, (L, L), 0)
    col_i = lax.broadcasted_iota(jnp.int32, (L, L), 1)
    causal = col_i <= row_i
    tril = causal.astype(F32)
    b_cols = jnp.dot(tril, jax.nn.log_sigmoid(g), preferred_element_type=F32, precision=lax.Precision.HIGHEST)
    b_rows = jnp.dot(jax.nn.log_sigmoid(gt), (row_i <= col_i).astype(F32), preferred_element_type=F32,
                     precision=lax.Precision.HIGHEST)

    m_prev_all = m_ref[...]
    masks = _head_lane_masks((L, D_MLSTM), MLSTM_DH)
    qf = q.astype(F32)
    kf = k.astype(F32)
    zero_q = jnp.zeros_like(q)
    qn = qf * n_ref[...]

    num_parts, den_col, mt_col, a_col, wk_col, dec_11, mnew_11 = [], [], [], [], [], [], []
    for h in range(H):
        b_col = b_cols[:, 4 + h:5 + h]
        b_row = b_rows[4 + h:5 + h, :]
        ig_row = gt[h:h + 1, :]
        ig_col = g[:, h:h + 1]
        m_prev = m_prev_all[:, h:h + 1]
        logw = jnp.where(causal, b_col - b_row + ig_row, -jnp.inf)
        inter = b_col + m_prev
        m_t = jnp.maximum(inter, jnp.max(logw, axis=1, keepdims=True))
        w = jnp.exp(logw - m_t)
        a = jnp.exp(inter - m_t)
        qh = jnp.where(masks[h], q, zero_q)
        s = _dot_nt(qh, k) * w
        num_parts.append(_dot(s.astype(MXU_DTYPE), v))
        den_col.append(jnp.sum(s, axis=1, keepdims=True) + a * jnp.sum(jnp.where(masks[h], qn, 0.0), axis=1, keepdims=True))
        mt_col.append(m_t)
        a_col.append(a)
        m_new = m_t[L - 1:L, :]
        b_last = b_col[L - 1:L, :]
        mnew_11.append(m_new)
        dec_11.append(jnp.exp(b_last + m_prev - m_new))
        wk_col.append(jnp.exp(b_last - b_col + ig_col - m_new))

    inter_num = _dot(q, c_ref[...].astype(MXU_DTYPE))
    num = _by_head(masks, num_parts) + _by_head(masks, a_col) * inter_num
    denom = _by_head(masks, [jnp.maximum(jnp.abs(den_col[h]), jnp.exp(-mt_col[h])) for h in range(H)])
    hm = num / denom
    hsq = hm * hm
    ms = _by_head(masks, [jnp.sum(jnp.where(masks[h], hsq, 0.0), axis=1, keepdims=True) * (1.0 / MLSTM_DH) for h in range(H)])
    hn = hm * lax.rsqrt(ms + EPS) * nrm_ref[...]
    y_ref[...] = (hn * jax.nn.sigmoid(opre_ref[...])).astype(y_ref.dtype)

    kw = kf * _by_head(masks, wk_col)
    c_add = _dot(kw.T.astype(MXU_DTYPE), v)
    masks_sq = _head_lane_masks((D_MLSTM, D_MLSTM), MLSTM_DH)
    rows_sq = lax.broadcasted_iota(jnp.int32, (D_MLSTM, D_MLSTM), 0)
    diag = jnp.zeros((D_MLSTM, D_MLSTM), jnp.bool_)
    for h in range(H):
        diag = diag | (masks_sq[h] & (rows_sq >= MLSTM_DH * h) & (rows_sq < MLSTM_DH * (h + 1)))
    lane1 = _head_lane_masks((1, D_MLSTM), MLSTM_DH)
    dec_lane = _by_head(lane1, dec_11)
    c_new = dec_lane * c_ref[...] + jnp.where(diag, c_add, 0.0)
    n_new = dec_lane * n_ref[...] + jnp.sum(kw, axis=0, keepdims=True)
    lane_m = lax.broadcasted_iota(jnp.int32, (1, 128), 1)
    m_new_all = m_prev_all
    for h in range(H):
        m_new_all = jnp.where(lane_m == h, mnew_11[h], m_new_all)
    c_ref[...] = c_new
    n_ref[...] = n_new
    m_ref[...] = m_new_all

    @pl.when(t == pl.num_programs(1) - 1)
    def _():
        cout_ref[...] = c_new
        nout_ref[...] = n_new
        mout_ref[...] = m_new_all


def _mlstm(qkv, opre, gates, lw, c0, n0, m0):
    b, t, _ = qkv.shape
    L = MLSTM_CHUNK
    seq = lambda w: pl.BlockSpec((None, L, w), lambda i, j: (i, j, 0))
    st = lambda r, w: pl.BlockSpec((None, r, w), lambda i, j: (i, 0, 0))
    return pl.pallas_call(
        functools.partial(_mlstm_body, L=L),
        out_shape=(jax.ShapeDtypeStruct((b, t, D_MLSTM), MXU_DTYPE),
                   jax.ShapeDtypeStruct((b, D_MLSTM, D_MLSTM), F32),
                   jax.ShapeDtypeStruct((b, 1, D_MLSTM), F32),
                   jax.ShapeDtypeStruct((b, 1, 128), F32)),
        grid=(b, t // L),
        in_specs=[seq(768), seq(256), seq(128), _const_spec((1, 128)), _const_spec((1, D_MLSTM)),
                  st(D_MLSTM, D_MLSTM), st(1, D_MLSTM), st(1, 128)],
        out_specs=(seq(256), st(D_MLSTM, D_MLSTM), st(1, D_MLSTM), st(1, 128)),
        scratch_shapes=[pltpu.VMEM((D_MLSTM, D_MLSTM), F32), pltpu.VMEM((1, D_MLSTM), F32), pltpu.VMEM((1, 128), F32)],
        compiler_params=_params(("parallel", "arbitrary")),
        name="mlstm",
    )(qkv, opre, gates, lw["gbias"], lw["mlstm_norm"], c0, n0, m0)


def _head_proj(o_heads, wuvp_ref, y_ref):
    for p in range(MLA_HEADS // 2):
        op = jnp.concatenate([o_heads(2 * p), o_heads(2 * p + 1)], axis=1).astype(MXU_DTYPE)
        y_ref[:, 128 * p:128 * p + 128] = _dot(op, wuvp_ref[p]).astype(y_ref.dtype)


ATTN_TQ = 256


def _attn_body(qlat_ref, r_ref, k_ref, kt_ref, wuvt_ref, y_ref, q2_ref, m_ref, l_ref, acc_ref, *, tq):
    i = pl.program_id(1)
    r = r_ref[...]
    lane = lax.broadcasted_iota(jnp.int32, (tq, 256), 1) & 127
    zero_r = jnp.zeros_like(r)
    for h in range(MLA_HEADS):
        q2_ref[h * tq:(h + 1) * tq, 0:256] = qlat_ref[h]
        q2_ref[h * tq:(h + 1) * tq, 256:512] = jnp.where((lane >= 16 * h) & (lane < 16 * h + 16), r, zero_r)
    m_ref[...] = jnp.full(m_ref.shape, -jnp.inf, F32)
    l_ref[...] = jnp.zeros(l_ref.shape, F32)
    acc_ref[...] = jnp.zeros(acc_ref.shape, F32)
    nq = MLA_HEADS * tq

    def block(j, masked):
        kb = k_ref[pl.ds(pl.multiple_of(j * tq, tq), tq), :]
        vt = kt_ref[j]
        st = _dot_nt(kb, q2_ref[...]) * SCORE_SCALE_LOG2
        if masked:
            kpos = lax.broadcasted_iota(jnp.int32, (tq, nq), 0)
            qpos = lax.broadcasted_iota(jnp.int32, (tq, nq), 1) & (tq - 1)
            st = jnp.where(kpos <= qpos, st, -jnp.inf)
        m_old = m_ref[...]
        m_new = jnp.maximum(m_old, jnp.max(st, axis=0, keepdims=True))
        alpha = jnp.exp2(m_old - m_new)
        p = jnp.exp2(st - m_new)
        l_ref[...] = alpha * l_ref[...] + jnp.sum(p, axis=0, keepdims=True)
        acc_ref[...] = alpha * acc_ref[...] + _dot(vt, p.astype(MXU_DTYPE))
        m_ref[...] = m_new

    def body(j, carry):
        block(j, False)
        return carry

    lax.fori_loop(0, i, body, 0)
    block(i, True)

    o = (acc_ref[...] / l_ref[...]).astype(MXU_DTYPE)
    yt = [_dot(wuvt_ref[h], o[:, h * tq:(h + 1) * tq]) for h in range(MLA_HEADS)]
    y_ref[...] = jnp.concatenate(yt, axis=0).T.astype(y_ref.dtype)


def _attn_prompt(qlat, r, kp, kt, lw, b, t):
    tq = ATTN_TQ
    nq = t // tq
    return pl.pallas_call(
        functools.partial(_attn_body, tq=tq),
        out_shape=jax.ShapeDtypeStruct((b * t, D_MLA), MXU_DTYPE),
        grid=(b, nq),
        in_specs=[pl.BlockSpec((MLA_HEADS, tq, 256), lambda i, j: (0, i * nq + j, 0)),
                  pl.BlockSpec((tq, 256), lambda i, j: (i * nq + j, 0)),
                  pl.BlockSpec((t, 512), lambda i, j: (i, 0)),
                  pl.BlockSpec((nq, KV_LORA, tq), lambda i, j: (i, 0, 0)),
                  _const_spec((MLA_HEADS, V_DIM, KV_LORA))],
        out_specs=pl.BlockSpec((tq, D_MLA), lambda i, j: (i * nq + j, 0)),
        scratch_shapes=[pltpu.VMEM((MLA_HEADS * tq, 512), MXU_DTYPE), pltpu.VMEM((1, MLA_HEADS * tq), F32),
                        pltpu.VMEM((1, MLA_HEADS * tq), F32), pltpu.VMEM((KV_LORA, MLA_HEADS * tq), F32)],
        compiler_params=_params(("parallel", "arbitrary")),
        name="attn_prompt",
    )(qlat, r, kp, kt, lw["w_uvt"])


NEW_PAD = 16
SAMPLE_CHAINS = 1


def _attn_sample_body(pt_ref, ql_ref, qr_ref, cn_ref, kn_ref, lat_hbm, krt_hbm, o_ref,
                      lat_buf, kr_buf, sem_lat, sem_kr, *, G, T, layer, n_pages):
    b = pl.program_id(0)
    n_rows = pl.num_programs(0)
    n_groups = n_pages // G

    def group_copies(row, grp, slot):
        out = []
        for g in range(G):
            page = pt_ref[row * n_pages + grp * G + g]
            out.append(pltpu.make_async_copy(lat_hbm.at[layer, page], lat_buf.at[slot, g], sem_lat.at[slot]))
            out.append(pltpu.make_async_copy(krt_hbm.at[layer, page], kr_buf.at[slot, g], sem_kr.at[slot]))
        return out

    def start(row, grp, slot):
        for c in group_copies(row, grp, slot):
            c.start()

    def wait(row, grp, slot):
        for c in group_copies(row, grp, slot):
            c.wait()

    @pl.when(b == 0)
    def _():
        start(0, 0, 0)

    ql = ql_ref[...]
    qr = qr_ref[...]

    def update(stats, s, vals):
        m_old, l_old, acc = stats
        m_new = jnp.maximum(m_old, jnp.max(s, axis=1, keepdims=True))
        alpha = jnp.exp(m_old - m_new)
        pf = jnp.exp(s - m_new)
        p = pf.astype(MXU_DTYPE)
        acc = alpha * acc
        off = 0
        for vb in vals:
            acc = acc + _dot(p[:, off:off + vb.shape[0]], vb)
            off += vb.shape[0]
        return m_new, alpha * l_old + jnp.sum(pf, axis=1, keepdims=True), acc

    rows = MLA_HEADS * T
    gc = G // SAMPLE_CHAINS
    init = (jnp.full((rows, 1), -jnp.inf, F32), jnp.zeros((rows, 1), F32), jnp.zeros((rows, KV_LORA), F32))
    chains = [init] * SAMPLE_CHAINS
    for grp in range(n_groups):
        slot = grp % 2
        wait(b, grp, slot)
        if grp + 1 < n_groups:
            start(b, grp + 1, 1 - slot)
        else:
            @pl.when(b + 1 < n_rows)
            def _():
                start(b + 1, 0, 1 - slot)
        for c in range(SAMPLE_CHAINS):
            cls = [lat_buf[slot, g].astype(MXU_DTYPE) for g in range(c * gc, (c + 1) * gc)]
            s = jnp.concatenate([_dot_nt(ql, cls[g]) + _dot(qr, kr_buf[slot, c * gc + g].astype(MXU_DTYPE))
                                 for g in range(gc)], axis=1)
            chains[c] = update(chains[c], s * ATTN_SCALE, cls)

    cn = cn_ref[...].astype(MXU_DTYPE)
    kn = kn_ref[...].astype(MXU_DTYPE)
    sn = (_dot_nt(ql, cn) + _dot_nt(qr, kn)) * ATTN_SCALE
    qpos = lax.broadcasted_iota(jnp.int32, sn.shape, 0) % T
    kpos = lax.broadcasted_iota(jnp.int32, sn.shape, 1)
    chains[0] = update(chains[0], jnp.where(kpos <= qpos, sn, -jnp.inf), [cn])
    m_fin = chains[0][0]
    for c in range(1, SAMPLE_CHAINS):
        m_fin = jnp.maximum(m_fin, chains[c][0])
    l_fin = sum(jnp.exp(mc - m_fin) * lc for mc, lc, _ in chains)
    acc_fin = sum(jnp.exp(mc - m_fin) * ac for mc, _, ac in chains)
    o_ref[...] = acc_fin / l_fin


def _attn_sample(ql, qr, cn, kn, cache_latent, cache_krope_t, page_table, layer, G):
    db, rows, _ = ql.shape
    T = rows // MLA_HEADS
    n_pages = page_table.shape[1]
    page = cache_latent.shape[2]
    G = _tile(n_pages, G)
    assert (n_pages // G) % 2 == 0, "the two-slot page pipeline needs an even number of page groups per row"
    per_b = lambda r, w: pl.BlockSpec((None, r, w), lambda i, pt: (i, 0, 0))
    hbm = pl.BlockSpec(memory_space=pl.ANY)
    grid_spec = pltpu.PrefetchScalarGridSpec(
        num_scalar_prefetch=1,
        grid=(db,),
        in_specs=[per_b(rows, KV_LORA), per_b(rows, QK_ROPE), per_b(NEW_PAD, KV_LORA), per_b(NEW_PAD, QK_ROPE), hbm, hbm],
        out_specs=pl.BlockSpec((None, rows, KV_LORA), lambda i, pt: (i, 0, 0)),
        scratch_shapes=[pltpu.VMEM((2, G, page, KV_LORA), F32), pltpu.VMEM((2, G, QK_ROPE, page), F32),
                        pltpu.SemaphoreType.DMA((2,)), pltpu.SemaphoreType.DMA((2,))],
    )
    return pl.pallas_call(
        functools.partial(_attn_sample_body, G=G, T=T, layer=layer, n_pages=n_pages),
        out_shape=jax.ShapeDtypeStruct((db, rows, KV_LORA), F32),
        grid_spec=grid_spec,
        compiler_params=_params(("arbitrary",)),
        name="attn_sample",
    )(page_table.reshape(-1), ql, qr, cn, kn, cache_latent, cache_krope_t)


def _uvproj_body(o_ref, wuvp_ref, y_ref):
    _head_proj(lambda h: o_ref[h], wuvp_ref, y_ref)


def _uvproj(o, lw):
    n = o.shape[1]
    return pl.pallas_call(
        _uvproj_body,
        out_shape=jax.ShapeDtypeStruct((n, D_MLA), MXU_DTYPE),
        grid=(1,),
        in_specs=[pl.BlockSpec((MLA_HEADS, n, 256), lambda i: (0, 0, 0)), _const_spec((MLA_HEADS // 2, 512, 128))],
        out_specs=pl.BlockSpec((n, D_MLA), lambda i: (0, 0)),
        compiler_params=_params(("arbitrary",)),
        name="uvproj",
    )(o, lw["w_uvp"])


FFN_CHUNK = 256


def _ffn_body(x_ref, yp_ref, ym_ref, ya_ref, p_ref, wout_ref, nf_ref, wg_ref, wu_ref, wd_ref, pn_ref, wpg_ref,
              wpp_ref, fn_ref, xo_ref, *maybe_yo, final):
    mix = (_dot(yp_ref[...], wout_ref[0:256, :]) + _dot(ym_ref[...], wout_ref[256:512, :])
           + _dot(ya_ref[...], wout_ref[512:1024, :]))
    x1 = x_ref[...] + mix
    hf = _rms(x1, nf_ref[...]).astype(MXU_DTYPE)
    acc = jnp.zeros_like(x1)
    for c in range(D_FF // FFN_CHUNK):
        sl = slice(c * FFN_CHUNK, (c + 1) * FFN_CHUNK)
        gate = _dot(hf, wg_ref[:, sl])
        up = _dot(hf, wu_ref[:, sl])
        acc = acc + _dot((gate * jax.nn.sigmoid(gate) * up).astype(MXU_DTYPE), wd_ref[sl, :])
    x2 = x1 + acc
    pg = jax.nn.sigmoid(_dot(_rms(x2, pn_ref[...]).astype(MXU_DTYPE), wpg_ref[...]))
    x3 = x2 + pg * _dot(p_ref[...].astype(MXU_DTYPE), wpp_ref[...])
    xo_ref[...] = x3
    if final:
        maybe_yo[0][...] = _rms(x3, fn_ref[...])


def _ffn(x, yp, ym, ya, p, lw, final_norm, final):
    n = x.shape[0]
    tm = _tile(n, 512)
    row = lambda w: pl.BlockSpec((tm, w), lambda i: (i, 0))
    out_shape = [jax.ShapeDtypeStruct((n, D_MODEL), F32)]
    out_specs = [row(D_MODEL)]
    if final:
        out_shape.append(jax.ShapeDtypeStruct((n, D_MODEL), F32))
        out_specs.append(row(D_MODEL))
    return pl.pallas_call(
        functools.partial(_ffn_body, final=final),
        out_shape=tuple(out_shape),
        grid=(n // tm,),
        in_specs=[row(D_MODEL), row(D_POOL), row(D_MLSTM), row(D_MLA), row(D_PLE),
                  _const_spec((D_MODEL, D_MODEL)), _const_spec((1, D_MODEL)),
                  _const_spec((D_MODEL, D_FF)), _const_spec((D_MODEL, D_FF)), _const_spec((D_FF, D_MODEL)),
                  _const_spec((1, D_MODEL)), _const_spec((D_MODEL, D_MODEL)), _const_spec((D_PLE, D_MODEL)),
                  _const_spec((1, D_MODEL))],
        out_specs=tuple(out_specs),
        compiler_params=_params(("parallel",)),
        name="ffn",
    )(x, yp, ym, ya, p, lw["w_out"], lw["norm_ffn"], lw["w_gate"], lw["w_up"], lw["w_down"], lw["ple_norm"],
      lw["w_ple_gate"], lw["w_ple_proj"], final_norm)


def _block_diag(blocks):
    n = len(blocks)
    r, c = blocks[0].shape
    out = jnp.zeros((n * r, n * c), blocks[0].dtype)
    for i, blk in enumerate(blocks):
        out = out.at[i * r:(i + 1) * r, i * c:(i + 1) * c].set(blk)
    return out


def _pack_layer(i, norm_mix, w_in, b_igate, b_fgate, pool_w, pool_scale, mlstm_norm, q_norm, w_uq, kv_norm, w_uk,
                w_uv, w_out, norm_ffn, w_gate, w_up, w_down, ple_norm, w_ple_gate, w_ple_proj):
    wi = w_in[i]
    kr = wi[:, 1928:1960]
    packed = jnp.concatenate([
        wi[:, 256:1024], wi[:, 1024:1280], wi[:, 0:256], wi[:, 1288:1672], wi[:, 1672:1928],
        wi[:, 1280:1288], jnp.zeros((D_MODEL, 120), wi.dtype),
        jnp.tile(kr[:, 0:16], (1, MLA_HEADS)), jnp.tile(kr[:, 16:32], (1, MLA_HEADS))], axis=1)
    wq = w_uq[i]
    w_uq_p = jnp.concatenate([wq[:, :, 0:64].reshape(Q_LORA, 512), wq[:, :, 64:80].reshape(Q_LORA, 128),
                              wq[:, :, 80:96].reshape(Q_LORA, 128)], axis=1)
    wk = w_uk[i]
    w_ukp = jnp.stack([_block_diag([wk[:, 2 * p, :].T, wk[:, 2 * p + 1, :].T]) for p in range(MLA_HEADS // 2)])
    wv = w_uv[i]
    w_uvp = jnp.stack([_block_diag([wv[:, 2 * p, :], wv[:, 2 * p + 1, :]]) for p in range(MLA_HEADS // 2)])
    w_uvt = jnp.transpose(wv, (1, 2, 0))
    gbias = jnp.concatenate([b_igate[i], b_fgate[i], jnp.zeros((120,), F32)]).reshape(1, 128)
    mx = lambda a: a.astype(MXU_DTYPE)
    return dict(
        norm_mix=norm_mix[i].reshape(1, -1), w_in=mx(packed), q_norm=q_norm[i].reshape(1, -1),
        kv_norm=kv_norm[i].reshape(1, -1), w_uq=mx(w_uq_p), w_ukp=mx(w_ukp), w_uvp=mx(w_uvp), w_uvt=mx(w_uvt),
        pool_w=mx(_block_diag([pool_w[i, g] for g in range(4)])), pool_scale=pool_scale[i].reshape(1, -1),
        gbias=gbias, mlstm_norm=mlstm_norm[i].reshape(1, -1),
        w_out=mx(w_out[i]), norm_ffn=norm_ffn[i].reshape(1, -1), w_gate=mx(w_gate[i]), w_up=mx(w_up[i]),
        w_down=mx(w_down[i]), ple_norm=ple_norm[i].reshape(1, -1), w_ple_gate=mx(w_ple_gate[i]),
        w_ple_proj=mx(w_ple_proj[i]))


def _rope_tables(pos):
    inv = jnp.power(ROPE_THETA, -jnp.arange(0, QK_ROPE, 2, dtype=F32) / QK_ROPE)
    ang = pos.astype(F32)[:, None] * inv[None, :]
    return jnp.tile(jnp.cos(ang), (1, MLA_HEADS)), jnp.tile(jnp.sin(ang), (1, MLA_HEADS))


def _diag_blocks(cbd):
    return jnp.stack([cbd[:, 64 * h:64 * h + 64, 64 * h:64 * h + 64] for h in range(MLSTM_HEADS)], axis=1)


def _pad_axis1(a, n, value=0.0):
    pad = [(0, 0)] * a.ndim
    pad[1] = (0, n - a.shape[1])
    return jnp.pad(a, pad, constant_values=value)


def _layer(x, p, lw, cos_tab, sin_tab, b, t, pos0, prefix, c0, n0, m0, attend, final_norm, final):
    qkv, opre, upool, gates, qlat, r, kp, ckv, kr, kt = _inproj(x, lw, cos_tab, sin_tab)
    upool3 = upool.reshape(b, t, D_POOL)
    y_pool = _pool(_pad_axis1(upool3, -(-t // 8) * 8), prefix, lw, pos0)[:, :t].reshape(b * t, D_POOL)
    pool_state = jnp.concatenate([prefix[:, 1:], upool3], axis=1)[:, -POOL_STATE:]

    L = MLSTM_CHUNK
    tp = -(-t // L) * L
    qkv3, opre3, gates3 = qkv.reshape(b, t, 768), opre.reshape(b, t, 256), gates.reshape(b, t, 128)
    if tp != t:
        lane = jnp.arange(128)
        padrow = jnp.where(lane < 4, NEG_BIG, jnp.where(lane < 8, 1e4, 0.0)).astype(F32)
        gates3 = jnp.concatenate([gates3, jnp.broadcast_to(padrow, (b, tp - t, 128))], axis=1)
        qkv3, opre3 = _pad_axis1(qkv3, tp), _pad_axis1(opre3, tp)
    y_ml, c_bd, n_new, m_new = _mlstm(qkv3, opre3, gates3, lw, c0, n0, m0)
    y_ml = y_ml[:, :t].reshape(b * t, D_MLSTM)

    y_mla = attend(qlat, r, kp, kt, ckv, kr)
    outs = _ffn(x, y_pool, y_ml, y_mla, p, lw, final_norm, final)
    states = (ckv.reshape(b, t, KV_LORA), kr.reshape(b, t, QK_ROPE), pool_state, _diag_blocks(c_bd),
              n_new.reshape(b, MLSTM_HEADS, MLSTM_DH), m_new[:, 0, :MLSTM_HEADS])
    return outs, states


SAMPLE_PAGES_PER_STEP = 32


def kernel(x_prompt, x_sample, cache_latent, cache_krope, state_pool, state_C, state_n, state_m, page_table, p_prompt, p_sample, norm_mix, w_in, b_igate, b_fgate, pool_w, pool_scale, mlstm_norm, q_norm, w_uq, kv_norm, w_uk, w_uv, w_out, norm_ffn, w_gate, w_up, w_down, ple_norm, w_ple_gate, w_ple_proj, final_norm):
    bp, tp_, _ = x_prompt.shape
    bs, ts, _ = x_sample.shape
    depth = w_in.shape[0]
    past_len = page_table.shape[1] * cache_latent.shape[2]
    cos_p, sin_p = _rope_tables(jnp.arange(tp_))
    cos_s, sin_s = _rope_tables(past_len + jnp.arange(ts))
    cos_s, sin_s = jnp.tile(cos_s, (bs, 1)), jnp.tile(sin_s, (bs, 1))
    fin = final_norm.reshape(1, -1)
    cache_krope_t = jnp.swapaxes(cache_krope, 2, 3)

    xp = x_prompt.reshape(bp * tp_, D_MODEL)
    xs = x_sample.reshape(bs * ts, D_MODEL)
    st_p, st_s = [], []
    yp = ys = None
    for i in range(depth):
        lw = _pack_layer(i, norm_mix, w_in, b_igate, b_fgate, pool_w, pool_scale, mlstm_norm, q_norm, w_uq, kv_norm,
                         w_uk, w_uv, w_out, norm_ffn, w_gate, w_up, w_down, ple_norm, w_ple_gate, w_ple_proj)
        final = i == depth - 1

        def attend_s(qlat, r, kp, kt, ckv, kr):
            ql = qlat.reshape(MLA_HEADS, bs, ts, KV_LORA).transpose(1, 0, 2, 3).reshape(bs, MLA_HEADS * ts, KV_LORA)
            qr = r.reshape(bs, ts, 2, MLA_HEADS, 16).transpose(0, 3, 1, 2, 4).reshape(bs, MLA_HEADS * ts, QK_ROPE)
            cn = _pad_axis1(ckv.reshape(bs, ts, KV_LORA), NEW_PAD)
            kn = _pad_axis1(kr.reshape(bs, ts, QK_ROPE), NEW_PAD)
            o = _attn_sample(ql, qr, cn, kn, cache_latent, cache_krope_t, page_table, i, SAMPLE_PAGES_PER_STEP)
            o = o.reshape(bs, MLA_HEADS, ts, KV_LORA).transpose(1, 0, 2, 3).reshape(MLA_HEADS, bs * ts, KV_LORA)
            return _uvproj(o, lw)

        prefix_s = jnp.concatenate([jnp.zeros((bs, 1, D_POOL), F32), state_pool[i]], axis=1)
        head_eye = jnp.eye(MLSTM_HEADS, dtype=bool)[None, :, None, :, None]
        c0 = jnp.where(head_eye, state_C[i][:, :, :, None, :], 0.0).reshape(bs, D_MLSTM, D_MLSTM)
        n0 = state_n[i].reshape(bs, 1, D_MLSTM)
        m0 = _pad_axis1(state_m[i], 128)[:, None, :]
        outs, ss = _layer(xs, p_sample[i].reshape(bs * ts, D_PLE), lw, cos_s, sin_s, bs, ts, past_len,
                          prefix_s, c0, n0, m0, attend_s, fin, final)
        xs = outs[0]
        if final:
            ys = outs[1]

        def attend_p(qlat, r, kp, kt, ckv, kr):
            return _attn_prompt(qlat, r, kp, kt, lw, bp, tp_)

        outs, sp = _layer(xp, p_prompt[i].reshape(bp * tp_, D_PLE), lw, cos_p, sin_p, bp, tp_, 0,
                          jnp.zeros((bp, POOL_HALO, D_POOL), F32), jnp.zeros((bp, D_MLSTM, D_MLSTM), F32),
                          jnp.zeros((bp, 1, D_MLSTM), F32), jnp.zeros((bp, 1, 128), F32), attend_p, fin, final)
        xp = outs[0]
        if final:
            yp = outs[1]
        st_p.append(sp)
        st_s.append(ss)

    stack = lambda sts, k: jnp.stack([s[k] for s in sts])
    return (yp.reshape(bp, tp_, D_MODEL), ys.reshape(bs, ts, D_MODEL),
            stack(st_p, 0), stack(st_p, 1), stack(st_p, 2), stack(st_p, 3), stack(st_p, 4), stack(st_p, 5),
            stack(st_s, 0), stack(st_s, 1), stack(st_s, 2), stack(st_s, 3), stack(st_s, 4), stack(st_s, 5))
```

```python
import functools
import math

import numpy as np
import jax
import jax.numpy as jnp
from jax import lax
from jax.experimental import pallas as pl
from jax.experimental.pallas import tpu as pltpu

D_MODEL = 1024
POOL_WINDOWS = (2, 4, 8, 16)
POOL_GROUP = 64
D_POOL = 256
POOL_STATE = 15
MLSTM_HEADS = 4
MLSTM_DH = 64
D_MLSTM = 256
MLA_HEADS = 8
QK_NOPE = 64
QK_ROPE = 32
V_DIM = 64
D_MLA = 512
Q_LORA = 384
KV_LORA = 256
ROPE_THETA = 10000.0
ATTN_SCALE = (QK_NOPE + QK_ROPE) ** -0.5
SCORE_SCALE_LOG2 = ATTN_SCALE * math.log2(math.e)
D_FF = 2816
D_PLE = 256
EPS = 1e-6

F32 = jnp.float32
MXU_DTYPE = jnp.bfloat16
POOL_HALO = 16
MLSTM_CHUNK = 128
MLSTM_ROWS = 4
MLSTM_ROWS_SHORT = 8
NEG_BIG = -1e30
VMEM_LIMIT = 56 * 1024 * 1024

_C_QKV = 0
_C_OPRE = 768
_C_UPOOL = 1024
_C_CQ = 1280
_C_CKV = 1664
_C_GATES = 1920
_C_KRT = 2048
_C_END = 2304


def _dot(a, b):
    return jnp.dot(a, b, preferred_element_type=F32)


def _dot_nt(a, b):
    return lax.dot_general(a, b, (((1,), (1,)), ((), ())), preferred_element_type=F32)


def _rms(x, g):
    return x * lax.rsqrt(jnp.mean(x * x, axis=-1, keepdims=True) + EPS) * g


def _tile(n, pref):
    t = min(n, pref)
    while n % t:
        t //= 2
    return t


def _const_spec(shape):
    nd = len(shape)
    return pl.BlockSpec(shape, lambda *_: (0,) * nd, pipeline_mode=pl.Buffered(1))


def _params(sem):
    return pltpu.CompilerParams(dimension_semantics=sem, vmem_limit_bytes=VMEM_LIMIT)


def _inproj_body(x_ref, nrm_ref, w_ref, qn_ref, kvn_ref, wuq_ref, wukp_ref, cos_ref, sin_ref,
                 qkv_ref, opre_ref, upool_ref, gates_ref, qlat_ref, r_ref, kp_ref, ckv_ref, kr_ref, kt_ref):
    hb = _rms(x_ref[...], nrm_ref[...]).astype(MXU_DTYPE)

    def seg(a, b):
        return _dot(hb, w_ref[:, a:b])

    qkv_ref[:, 0:256] = seg(_C_QKV, _C_QKV + 256).astype(qkv_ref.dtype)
    qkv_ref[:, 256:512] = (seg(_C_QKV + 256, _C_QKV + 512) * (MLSTM_DH ** -0.5)).astype(qkv_ref.dtype)
    qkv_ref[:, 512:768] = seg(_C_QKV + 512, _C_QKV + 768).astype(qkv_ref.dtype)
    opre_ref[...] = seg(_C_OPRE, _C_UPOOL)
    upool_ref[...] = seg(_C_UPOOL, _C_CQ)
    gates_ref[...] = seg(_C_GATES, _C_KRT)

    cos = cos_ref[...]
    sin = sin_ref[...]

    cqn = _rms(seg(_C_CQ, _C_CKV), qn_ref[...]).astype(MXU_DTYPE)
    qf = _dot(cqn, wuq_ref[...])
    r1 = qf[:, 512:640]
    r2 = qf[:, 640:768]
    r_ref[:, 0:128] = (r1 * cos - r2 * sin).astype(r_ref.dtype)
    r_ref[:, 128:256] = (r2 * cos + r1 * sin).astype(r_ref.dtype)
    for p in range(MLA_HEADS // 2):
        ql = _dot(qf[:, 128 * p:128 * p + 128].astype(MXU_DTYPE), wukp_ref[p])
        qlat_ref[2 * p] = ql[:, 0:256].astype(qlat_ref.dtype)
        qlat_ref[2 * p + 1] = ql[:, 256:512].astype(qlat_ref.dtype)

    ckv = _rms(seg(_C_CKV, _C_GATES), kvn_ref[...])
    ckv_ref[...] = ckv
    kp_ref[:, 0:256] = ckv.astype(kp_ref.dtype)
    for c in range(kt_ref.shape[0]):
        kt_ref[c] = ckv[c * kt_ref.shape[2]:(c + 1) * kt_ref.shape[2], :].T.astype(kt_ref.dtype)
    krt = seg(_C_KRT, _C_END)
    k1 = krt[:, 0:128]
    k2 = krt[:, 128:256]
    kt1 = k1 * cos - k2 * sin
    kt2 = k2 * cos + k1 * sin
    kp_ref[:, 256:384] = kt1.astype(kp_ref.dtype)
    kp_ref[:, 384:512] = kt2.astype(kp_ref.dtype)
    kr_ref[...] = jnp.concatenate([kt1[:, 0:16], kt2[:, 0:16]], axis=1)


def _inproj(x, lw, cos_tab, sin_tab):
    n = x.shape[0]
    tm = _tile(n, 512)
    tkb = min(ATTN_TQ, tm)
    ntab = cos_tab.shape[0] // tm
    row = lambda w: pl.BlockSpec((tm, w), lambda i: (i, 0))
    tab = pl.BlockSpec((tm, 128), lambda i: (i % ntab, 0))
    out_shape = (
        jax.ShapeDtypeStruct((n, 768), MXU_DTYPE),
        jax.ShapeDtypeStruct((n, 256), F32),
        jax.ShapeDtypeStruct((n, 256), F32),
        jax.ShapeDtypeStruct((n, 128), F32),
        jax.ShapeDtypeStruct((MLA_HEADS, n, 256), MXU_DTYPE),
        jax.ShapeDtypeStruct((n, 256), MXU_DTYPE),
        jax.ShapeDtypeStruct((n, 512), MXU_DTYPE),
        jax.ShapeDtypeStruct((n, 256), F32),
        jax.ShapeDtypeStruct((n, 32), F32),
        jax.ShapeDtypeStruct((n // tkb, KV_LORA, tkb), MXU_DTYPE),
    )
    out_specs = (row(768), row(256), row(256), row(128),
                 pl.BlockSpec((MLA_HEADS, tm, 256), lambda i: (0, i, 0)),
                 row(256), row(512), row(256), row(32),
                 pl.BlockSpec((tm // tkb, KV_LORA, tkb), lambda i: (i, 0, 0)))
    return pl.pallas_call(
        _inproj_body,
        out_shape=out_shape,
        grid=(n // tm,),
        in_specs=[row(D_MODEL), _const_spec((1, D_MODEL)), _const_spec((D_MODEL, _C_END)),
                  _const_spec((1, Q_LORA)), _const_spec((1, KV_LORA)), _const_spec((Q_LORA, 768)),
                  _const_spec((MLA_HEADS // 2, 128, 512)), tab, tab],
        out_specs=out_specs,
        compiler_params=_params(("parallel",)),
        name="inproj",
    )(x, lw["norm_mix"], lw["w_in"], lw["q_norm"], lw["kv_norm"], lw["w_uq"], lw["w_ukp"], cos_tab, sin_tab)


def _pool_body(pre_ref, u_ref, w_ref, sc_ref, y_ref, halo_ref, *, pos0, tc):
    t = pl.program_id(1)

    @pl.when(t == 0)
    def _():
        halo_ref[...] = pre_ref[...]

    u = u_ref[...]
    full = jnp.concatenate([halo_ref[...], u], axis=0)
    a2 = full + pltpu.roll(full, 1, 0)
    a4 = a2 + pltpu.roll(a2, 2, 0)
    a8 = a4 + pltpu.roll(a4, 4, 0)
    a16 = a8 + pltpu.roll(a8, 8, 0)
    lane = lax.broadcasted_iota(jnp.int32, (tc, D_POOL), 1)
    g0, g1, g2 = lane < 64, lane < 128, lane < 192
    win = jnp.where(g0, a2[POOL_HALO:], jnp.where(g1, a4[POOL_HALO:], jnp.where(g2, a8[POOL_HALO:], a16[POOL_HALO:])))
    wsz = jnp.where(g0, 2, jnp.where(g1, 4, jnp.where(g2, 8, 16)))
    pos = lax.broadcasted_iota(jnp.int32, (tc, D_POOL), 0) + (pos0 + t * tc)
    cnt = jnp.minimum(pos + 1, wsz).astype(F32)
    d = (win / cnt - u).astype(MXU_DTYPE)
    y_ref[...] = (_dot(d, w_ref[...]) * sc_ref[...]).astype(y_ref.dtype)
    halo_ref[...] = full[tc:, :]


def _pool(u, prefix, lw, pos0):
    b, t, _ = u.shape
    tc = _tile(t, 512)
    return pl.pallas_call(
        functools.partial(_pool_body, pos0=pos0, tc=tc),
        out_shape=jax.ShapeDtypeStruct((b, t, D_POOL), MXU_DTYPE),
        grid=(b, t // tc),
        in_specs=[pl.BlockSpec((None, POOL_HALO, D_POOL), lambda i, j: (i, 0, 0)),
                  pl.BlockSpec((None, tc, D_POOL), lambda i, j: (i, j, 0)),
                  _const_spec((D_POOL, D_POOL)), _const_spec((1, D_POOL))],
        out_specs=pl.BlockSpec((None, tc, D_POOL), lambda i, j: (i, j, 0)),
        scratch_shapes=[pltpu.VMEM((POOL_HALO, D_POOL), F32)],
        compiler_params=_params(("parallel", "arbitrary")),
        name="pool",
    )(prefix, u, lw["pool_w"], lw["pool_scale"])


def _head_lane_masks(shape, width):
    lane = lax.broadcasted_iota(jnp.int32, shape, len(shape) - 1)
    return [(lane >= width * h) & (lane < width * (h + 1)) for h in range(MLSTM_HEADS)]


def _by_head(masks, vals):
    out = jnp.where(masks[0], vals[0], 0.0)
    for h in range(1, MLSTM_HEADS):
        out = jnp.where(masks[h], vals[h], out)
    return out


def _mlstm_body(qkv_ref, opre_ref, gates_ref, gbias_ref, nrm_ref, c0_ref, n0_ref, m0_ref,
                y_ref, cout_ref, nout_ref, mout_ref, c_ref, n_ref, m_ref, *, L, NB):
    t = pl.program_id(1)

    @pl.when(t == 0)
    def _():
        c_ref[...] = c0_ref[...]
        n_ref[...] = n0_ref[...]
        m_ref[...] = m0_ref[...]

    for r in range(NB):
        _mlstm_row(qkv_ref.at[r], opre_ref.at[r], gates_ref.at[r], gbias_ref, nrm_ref, y_ref.at[r], c_ref.at[r],
                   n_ref.at[r], m_ref.at[r], L=L)

    @pl.when(t == pl.num_programs(1) - 1)
    def _():
        cout_ref[...] = c_ref[...]
        nout_ref[...] = n_ref[...]
        mout_ref[...] = m_ref[...]


def _mlstm_row(qkv_ref, opre_ref, gates_ref, gbias_ref, nrm_ref, y_ref, c_ref, n_ref, m_ref, *, L):
    H, DH = MLSTM_HEADS, MLSTM_DH
    q = qkv_ref[:, 0:256]
    k = qkv_ref[:, 256:512]
    v = qkv_ref[:, 512:768]
    qt = q.astype(F32).T.astype(MXU_DTYPE)
    vt = v.astype(F32).T
    vtb = vt.astype(MXU_DTYPE)
    g = gates_ref[...] + gbias_ref[...]
    gt = g.T
    src_i = lax.broadcasted_iota(jnp.int32, (L, L), 0)
    tgt_i = lax.broadcasted_iota(jnp.int32, (L, L), 1)
    causal = src_i <= tgt_i
    b_cols = jnp.dot((tgt_i <= src_i).astype(F32), jax.nn.log_sigmoid(g), preferred_element_type=F32,
                     precision=lax.Precision.HIGHEST)
    b_rows = jnp.dot(jax.nn.log_sigmoid(gt), causal.astype(F32), preferred_element_type=F32,
                     precision=lax.Precision.HIGHEST)

    m_prev_all = m_ref[...]
    masks = _head_lane_masks((L, D_MLSTM), DH)
    zero_k = jnp.zeros_like(k)
    ct = c_ref[...]
    inter_num = _dot(ct.astype(MXU_DTYPE), qt)
    head_row = lax.broadcasted_iota(jnp.int32, (8, D_MLSTM), 0)
    head_lane = lax.broadcasted_iota(jnp.int32, (8, D_MLSTM), 1) // DH
    n8 = jnp.where(head_row == head_lane, jnp.broadcast_to(n_ref[...], (8, D_MLSTM)), 0.0)
    qn = _dot(n8.astype(MXU_DTYPE), qt)

    hn_parts, vw_parts, wk_rows, dec_11, mnew_11 = [], [], [], [], []
    for h in range(H):
        rs = slice(DH * h, DH * (h + 1))
        b_row = b_rows[4 + h:5 + h, :]
        ig_row = gt[h:h + 1, :]
        m_prev = m_prev_all[:, h:h + 1]
        src_col = b_cols[:, 4 + h:5 + h] - g[:, h:h + 1]
        logw = jnp.where(causal, b_row - src_col, -jnp.inf)
        inter = b_row + m_prev
        m_t = jnp.maximum(inter, jnp.max(logw, axis=0, keepdims=True))
        w = jnp.exp(logw - m_t)
        a = jnp.exp(inter - m_t)
        st = _dot_nt(jnp.where(masks[h], k, zero_k), q) * w
        num = _dot(vtb[rs, :], st.astype(MXU_DTYPE)) + a * inter_num[rs, :]
        den = jnp.sum(st, axis=0, keepdims=True) + a * qn[h:h + 1, :]
        hm = num / jnp.maximum(jnp.abs(den), jnp.exp(-m_t))
        ms = jnp.mean(hm * hm, axis=0, keepdims=True)
        hn_parts.append(hm * lax.rsqrt(ms + EPS))
        m_new = m_t[:, L - 1:L]
        b_last = b_row[:, L - 1:L]
        wk = jnp.exp(b_last - b_row + ig_row - m_new)
        mnew_11.append(m_new)
        dec_11.append(jnp.exp(b_last + m_prev - m_new))
        wk_rows.append(wk)
        vw_parts.append(vt[rs, :] * wk)

    hn = jnp.concatenate(hn_parts, axis=0) * nrm_ref[:, 0:L]
    y_ref[...] = (hn * jax.nn.sigmoid(opre_ref[...].T)).T.astype(y_ref.dtype)

    c_add = _dot(jnp.concatenate(vw_parts, axis=0).astype(MXU_DTYPE), k)
    wk8 = jnp.concatenate(wk_rows + [jnp.zeros((8 - H, L), F32)], axis=0).astype(MXU_DTYPE)
    n_add8 = _dot(wk8, k)
    masks_sq = _head_lane_masks((D_MLSTM, D_MLSTM), MLSTM_DH)
    rows_sq = lax.broadcasted_iota(jnp.int32, (D_MLSTM, D_MLSTM), 0)
    diag = jnp.zeros((D_MLSTM, D_MLSTM), jnp.bool_)
    for h in range(H):
        diag = diag | (masks_sq[h] & (rows_sq >= MLSTM_DH * h) & (rows_sq < MLSTM_DH * (h + 1)))
    lane1 = _head_lane_masks((1, D_MLSTM), MLSTM_DH)
    dec_lane = _by_head(lane1, dec_11)
    c_new = dec_lane * ct + jnp.where(diag, c_add, 0.0)
    n_new = dec_lane * n_ref[...] + _by_head(lane1, [n_add8[h:h + 1, :] for h in range(H)])
    lane_m = lax.broadcasted_iota(jnp.int32, (1, 128), 1)
    m_new_all = m_prev_all
    for h in range(H):
        m_new_all = jnp.where(lane_m == h, mnew_11[h], m_new_all)
    c_ref[...] = c_new
    n_ref[...] = n_new
    m_ref[...] = m_new_all


def _mlstm(qkv, opre, gates, lw, c0, n0, m0, L):
    b, t, _ = qkv.shape
    NB = _tile(b, MLSTM_ROWS if L == MLSTM_CHUNK else MLSTM_ROWS_SHORT)
    seq = lambda w: pl.BlockSpec((NB, L, w), lambda i, j: (i, j, 0))
    st = lambda r, w: pl.BlockSpec((NB, r, w), lambda i, j: (i, 0, 0))
    return pl.pallas_call(
        functools.partial(_mlstm_body, L=L, NB=NB),
        out_shape=(jax.ShapeDtypeStruct((b, t, D_MLSTM), MXU_DTYPE),
                   jax.ShapeDtypeStruct((b, D_MLSTM, D_MLSTM), F32),
                   jax.ShapeDtypeStruct((b, 1, D_MLSTM), F32),
                   jax.ShapeDtypeStruct((b, 1, 128), F32)),
        grid=(b // NB, t // L),
        in_specs=[seq(768), seq(256), seq(128), _const_spec((1, 128)), _const_spec((D_MLSTM, MLSTM_CHUNK)),
                  st(D_MLSTM, D_MLSTM), st(1, D_MLSTM), st(1, 128)],
        out_specs=(seq(256), st(D_MLSTM, D_MLSTM), st(1, D_MLSTM), st(1, 128)),
        scratch_shapes=[pltpu.VMEM((NB, D_MLSTM, D_MLSTM), F32), pltpu.VMEM((NB, 1, D_MLSTM), F32),
                        pltpu.VMEM((NB, 1, 128), F32)],
        compiler_params=_params(("parallel", "arbitrary")),
        name="mlstm",
    )(qkv, opre, gates, lw["gbias"], lw["mlstm_norm_t"], c0, n0, m0)


def _head_proj(o_heads, wuvp_ref, y_ref):
    for p in range(MLA_HEADS // 2):
        op = jnp.concatenate([o_heads(2 * p), o_heads(2 * p + 1)], axis=1).astype(MXU_DTYPE)
        y_ref[:, 128 * p:128 * p + 128] = _dot(op, wuvp_ref[p]).astype(y_ref.dtype)


ATTN_TQ = 256


def _attn_body(qlat_ref, r_ref, k_ref, kt_ref, wuvt_ref, y_ref, q2_ref, m_ref, l_ref, acc_ref, *, tq):
    i = pl.program_id(1)
    r = r_ref[...]
    lane = lax.broadcasted_iota(jnp.int32, (tq, 256), 1) & 127
    zero_r = jnp.zeros_like(r)
    for h in range(MLA_HEADS):
        q2_ref[h * tq:(h + 1) * tq, 0:256] = qlat_ref[h]
        q2_ref[h * tq:(h + 1) * tq, 256:512] = jnp.where((lane >= 16 * h) & (lane < 16 * h + 16), r, zero_r)
    m_ref[...] = jnp.full(m_ref.shape, -jnp.inf, F32)
    l_ref[...] = jnp.zeros(l_ref.shape, F32)
    acc_ref[...] = jnp.zeros(acc_ref.shape, F32)
    nq = MLA_HEADS * tq

    def block(j, nblk, masked):
        kb = k_ref[pl.ds(pl.multiple_of(j * tq, tq), nblk * tq), :]
        vt = kt_ref[j] if nblk == 1 else jnp.concatenate([kt_ref[j + c] for c in range(nblk)], axis=1)
        st = _dot_nt(kb, q2_ref[...]) * SCORE_SCALE_LOG2
        if masked:
            kpos = lax.broadcasted_iota(jnp.int32, (tq, nq), 0)
            qpos = lax.broadcasted_iota(jnp.int32, (tq, nq), 1) & (tq - 1)
            st = jnp.where(kpos <= qpos, st, -jnp.inf)
        m_old = m_ref[...]
        m_new = jnp.maximum(m_old, jnp.max(st, axis=0, keepdims=True))
        alpha = jnp.exp2(m_old - m_new)
        p = jnp.exp2(st - m_new)
        l_ref[...] = alpha * l_ref[...] + jnp.sum(p, axis=0, keepdims=True)
        acc_ref[...] = alpha * acc_ref[...] + _dot(vt, p.astype(MXU_DTYPE))
        m_ref[...] = m_new

    def body(j, carry):
        block(2 * j, 2, False)
        return carry

    lax.fori_loop(0, i // 2, body, 0)

    @pl.when(i % 2 == 1)
    def _():
        block(i - 1, 1, False)

    block(i, 1, True)

    o = (acc_ref[...] / l_ref[...]).astype(MXU_DTYPE)
    yt = [_dot(wuvt_ref[h], o[:, h * tq:(h + 1) * tq]) for h in range(MLA_HEADS)]
    y_ref[...] = jnp.concatenate(yt, axis=0).T.astype(y_ref.dtype)


def _attn_prompt(qlat, r, kp, kt, lw, b, t):
    tq = ATTN_TQ
    nq = t // tq
    return pl.pallas_call(
        functools.partial(_attn_body, tq=tq),
        out_shape=jax.ShapeDtypeStruct((b * t, D_MLA), MXU_DTYPE),
        grid=(b, nq),
        in_specs=[pl.BlockSpec((MLA_HEADS, tq, 256), lambda i, j: (0, i * nq + j, 0)),
                  pl.BlockSpec((tq, 256), lambda i, j: (i * nq + j, 0)),
                  pl.BlockSpec((t, 512), lambda i, j: (i, 0)),
                  pl.BlockSpec((nq, KV_LORA, tq), lambda i, j: (i, 0, 0)),
                  _const_spec((MLA_HEADS, V_DIM, KV_LORA))],
        out_specs=pl.BlockSpec((tq, D_MLA), lambda i, j: (i * nq + j, 0)),
        scratch_shapes=[pltpu.VMEM((MLA_HEADS * tq, 512), MXU_DTYPE), pltpu.VMEM((1, MLA_HEADS * tq), F32),
                        pltpu.VMEM((1, MLA_HEADS * tq), F32), pltpu.VMEM((KV_LORA, MLA_HEADS * tq), F32)],
        compiler_params=_params(("parallel", "arbitrary")),
        name="attn_prompt",
    )(qlat, r, kp, kt, lw["w_uvt"])


NEW_PAD = 16
SAMPLE_SLOTS = 4


def _attn_sample_body(pt_ref, ql_ref, qr_ref, cn_ref, kn_ref, lat_hbm, krt_hbm, o_ref,
                      lat_buf, kr_buf, sem_lat, sem_kr, *, G, T, layer, n_pages):
    b = pl.program_id(0)
    n_rows = pl.num_programs(0)
    n_groups = n_pages // G

    def group_copies(row, grp, slot):
        out = []
        for g in range(G):
            page = pt_ref[row * n_pages + grp * G + g]
            out.append(pltpu.make_async_copy(lat_hbm.at[layer, page], lat_buf.at[slot, g], sem_lat.at[slot]))
            out.append(pltpu.make_async_copy(krt_hbm.at[layer, page], kr_buf.at[slot, g], sem_kr.at[slot]))
        return out

    def start(row, grp, slot):
        for c in group_copies(row, grp, slot):
            c.start()

    def wait(row, grp, slot):
        for c in group_copies(row, grp, slot):
            c.wait()

    NS = SAMPLE_SLOTS

    @pl.when(b == 0)
    def _():
        for g in range(NS - 1):
            start(0, g, g)

    def start_ahead(g):
        nxt = g + NS - 1
        if nxt < n_groups:
            start(b, nxt, nxt % NS)
        else:
            @pl.when(b + 1 < n_rows)
            def _():
                start(b + 1, nxt - n_groups, nxt % NS)

    ql = ql_ref[...]
    qr = qr_ref[...]

    def values(slot):
        return [lat_buf[slot, g].astype(MXU_DTYPE) for g in range(G)]

    def scores(slot):
        cls = values(slot)
        return jnp.concatenate([_dot_nt(ql, cls[g]) + _dot(qr, kr_buf[slot, g].astype(MXU_DTYPE))
                                for g in range(G)], axis=1) * ATTN_SCALE

    def update(stats, s, vals):
        m_old, l_old, acc = stats
        m_new = jnp.maximum(m_old, jnp.max(s, axis=1, keepdims=True))
        alpha = jnp.exp(m_old - m_new)
        pf = jnp.exp(s - m_new)
        p = pf.astype(MXU_DTYPE)
        acc = alpha * acc
        off = 0
        for vb in vals:
            acc = acc + _dot(p[:, off:off + vb.shape[0]], vb)
            off += vb.shape[0]
        return m_new, alpha * l_old + jnp.sum(pf, axis=1, keepdims=True), acc

    rows = MLA_HEADS * T
    stats = (jnp.full((rows, 1), -jnp.inf, F32), jnp.zeros((rows, 1), F32), jnp.zeros((rows, KV_LORA), F32))
    wait(b, 0, 0)
    s_cur = scores(0)
    for grp in range(n_groups):
        if grp + 1 < n_groups:
            wait(b, grp + 1, (grp + 1) % NS)
        start_ahead(grp)
        if grp + 1 < n_groups:
            s_next = scores((grp + 1) % NS)
        stats = update(stats, s_cur, values(grp % NS))
        if grp + 1 < n_groups:
            s_cur = s_next

    cn = cn_ref[...].astype(MXU_DTYPE)
    kn = kn_ref[...].astype(MXU_DTYPE)
    sn = (_dot_nt(ql, cn) + _dot_nt(qr, kn)) * ATTN_SCALE
    qpos = lax.broadcasted_iota(jnp.int32, sn.shape, 0) % T
    kpos = lax.broadcasted_iota(jnp.int32, sn.shape, 1)
    _, l_fin, acc_fin = update(stats, jnp.where(kpos <= qpos, sn, -jnp.inf), [cn])
    o_ref[...] = acc_fin / l_fin


def _attn_sample(ql, qr, cn, kn, cache_latent, cache_krope_t, page_table, layer, G):
    db, rows, _ = ql.shape
    T = rows // MLA_HEADS
    n_pages = page_table.shape[1]
    page = cache_latent.shape[2]
    G = _tile(n_pages, G)
    NS = SAMPLE_SLOTS
    assert (n_pages // G) % NS == 0, "page groups per row must be a multiple of the number of VMEM slots"
    per_b = lambda r, w: pl.BlockSpec((None, r, w), lambda i, pt: (i, 0, 0))
    hbm = pl.BlockSpec(memory_space=pl.ANY)
    grid_spec = pltpu.PrefetchScalarGridSpec(
        num_scalar_prefetch=1,
        grid=(db,),
        in_specs=[per_b(rows, KV_LORA), per_b(rows, QK_ROPE), per_b(NEW_PAD, KV_LORA), per_b(NEW_PAD, QK_ROPE), hbm, hbm],
        out_specs=pl.BlockSpec((None, rows, KV_LORA), lambda i, pt: (i, 0, 0)),
        scratch_shapes=[pltpu.VMEM((NS, G, page, KV_LORA), F32), pltpu.VMEM((NS, G, QK_ROPE, page), F32),
                        pltpu.SemaphoreType.DMA((NS,)), pltpu.SemaphoreType.DMA((NS,))],
    )
    return pl.pallas_call(
        functools.partial(_attn_sample_body, G=G, T=T, layer=layer, n_pages=n_pages),
        out_shape=jax.ShapeDtypeStruct((db, rows, KV_LORA), F32),
        grid_spec=grid_spec,
        compiler_params=_params(("arbitrary",)),
        name="attn_sample",
    )(page_table.reshape(-1), ql, qr, cn, kn, cache_latent, cache_krope_t)


def _uvproj_body(o_ref, wuvp_ref, y_ref):
    _head_proj(lambda h: o_ref[h], wuvp_ref, y_ref)


def _uvproj(o, lw):
    n = o.shape[1]
    return pl.pallas_call(
        _uvproj_body,
        out_shape=jax.ShapeDtypeStruct((n, D_MLA), MXU_DTYPE),
        grid=(1,),
        in_specs=[pl.BlockSpec((MLA_HEADS, n, 256), lambda i: (0, 0, 0)), _const_spec((MLA_HEADS // 2, 512, 128))],
        out_specs=pl.BlockSpec((n, D_MLA), lambda i: (0, 0)),
        compiler_params=_params(("arbitrary",)),
        name="uvproj",
    )(o, lw["w_uvp"])


FFN_CHUNK = 256


def _ffn_body(x_ref, yp_ref, ym_ref, ya_ref, p_ref, wout_ref, nf_ref, wg_ref, wu_ref, wd_ref, pn_ref, wpg_ref,
              wpp_ref, fn_ref, xo_ref, *maybe_yo, final):
    mix = (_dot(yp_ref[...], wout_ref[0:256, :]) + _dot(ym_ref[...], wout_ref[256:512, :])
           + _dot(ya_ref[...], wout_ref[512:1024, :]))
    x1 = x_ref[...] + mix
    hf = _rms(x1, nf_ref[...]).astype(MXU_DTYPE)
    acc = jnp.zeros_like(x1)
    for c in range(D_FF // FFN_CHUNK):
        sl = slice(c * FFN_CHUNK, (c + 1) * FFN_CHUNK)
        gate = _dot(hf, wg_ref[:, sl])
        up = _dot(hf, wu_ref[:, sl])
        acc = acc + _dot((gate * jax.nn.sigmoid(gate) * up).astype(MXU_DTYPE), wd_ref[sl, :])
    x2 = x1 + acc
    pg = jax.nn.sigmoid(_dot(_rms(x2, pn_ref[...]).astype(MXU_DTYPE), wpg_ref[...]))
    x3 = x2 + pg * _dot(p_ref[...].astype(MXU_DTYPE), wpp_ref[...])
    xo_ref[...] = x3
    if final:
        maybe_yo[0][...] = _rms(x3, fn_ref[...])


def _ffn(x, yp, ym, ya, p, lw, final_norm, final):
    n = x.shape[0]
    tm = _tile(n, 512)
    row = lambda w: pl.BlockSpec((tm, w), lambda i: (i, 0))
    out_shape = [jax.ShapeDtypeStruct((n, D_MODEL), F32)]
    out_specs = [row(D_MODEL)]
    if final:
        out_shape.append(jax.ShapeDtypeStruct((n, D_MODEL), F32))
        out_specs.append(row(D_MODEL))
    return pl.pallas_call(
        functools.partial(_ffn_body, final=final),
        out_shape=tuple(out_shape),
        grid=(n // tm,),
        in_specs=[row(D_MODEL), row(D_POOL), row(D_MLSTM), row(D_MLA), row(D_PLE),
                  _const_spec((D_MODEL, D_MODEL)), _const_spec((1, D_MODEL)),
                  _const_spec((D_MODEL, D_FF)), _const_spec((D_MODEL, D_FF)), _const_spec((D_FF, D_MODEL)),
                  _const_spec((1, D_MODEL)), _const_spec((D_MODEL, D_MODEL)), _const_spec((D_PLE, D_MODEL)),
                  _const_spec((1, D_MODEL))],
        out_specs=tuple(out_specs),
        compiler_params=_params(("parallel",)),
        name="ffn",
    )(x, yp, ym, ya, p, lw["w_out"], lw["norm_ffn"], lw["w_gate"], lw["w_up"], lw["w_down"], lw["ple_norm"],
      lw["w_ple_gate"], lw["w_ple_proj"], final_norm)


def _block_diag(blocks):
    n = len(blocks)
    r, c = blocks[0].shape
    out = jnp.zeros((n * r, n * c), blocks[0].dtype)
    for i, blk in enumerate(blocks):
        out = out.at[i * r:(i + 1) * r, i * c:(i + 1) * c].set(blk)
    return out


def _pack_layer(i, norm_mix, w_in, b_igate, b_fgate, pool_w, pool_scale, mlstm_norm, q_norm, w_uq, kv_norm, w_uk,
                w_uv, w_out, norm_ffn, w_gate, w_up, w_down, ple_norm, w_ple_gate, w_ple_proj):
    wi = w_in[i]
    kr = wi[:, 1928:1960]
    packed = jnp.concatenate([
        wi[:, 256:1024], wi[:, 1024:1280], wi[:, 0:256], wi[:, 1288:1672], wi[:, 1672:1928],
        wi[:, 1280:1288], jnp.zeros((D_MODEL, 120), wi.dtype),
        jnp.tile(kr[:, 0:16], (1, MLA_HEADS)), jnp.tile(kr[:, 16:32], (1, MLA_HEADS))], axis=1)
    wq = w_uq[i]
    w_uq_p = jnp.concatenate([wq[:, :, 0:64].reshape(Q_LORA, 512), wq[:, :, 64:80].reshape(Q_LORA, 128),
                              wq[:, :, 80:96].reshape(Q_LORA, 128)], axis=1)
    wk = w_uk[i]
    w_ukp = jnp.stack([_block_diag([wk[:, 2 * p, :].T, wk[:, 2 * p + 1, :].T]) for p in range(MLA_HEADS // 2)])
    wv = w_uv[i]
    w_uvp = jnp.stack([_block_diag([wv[:, 2 * p, :], wv[:, 2 * p + 1, :]]) for p in range(MLA_HEADS // 2)])
    w_uvt = jnp.transpose(wv, (1, 2, 0))
    gbias = jnp.concatenate([b_igate[i], b_fgate[i], jnp.zeros((120,), F32)]).reshape(1, 128)
    mx = lambda a: a.astype(MXU_DTYPE)
    return dict(
        norm_mix=norm_mix[i].reshape(1, -1), w_in=mx(packed), q_norm=q_norm[i].reshape(1, -1),
        kv_norm=kv_norm[i].reshape(1, -1), w_uq=mx(w_uq_p), w_ukp=mx(w_ukp), w_uvp=mx(w_uvp), w_uvt=mx(w_uvt),
        pool_w=mx(_block_diag([pool_w[i, g] for g in range(4)])), pool_scale=pool_scale[i].reshape(1, -1),
        gbias=gbias, mlstm_norm_t=jnp.broadcast_to(mlstm_norm[i][:, None], (D_MLSTM, MLSTM_CHUNK)),
        w_out=mx(w_out[i]), norm_ffn=norm_ffn[i].reshape(1, -1), w_gate=mx(w_gate[i]), w_up=mx(w_up[i]),
        w_down=mx(w_down[i]), ple_norm=ple_norm[i].reshape(1, -1), w_ple_gate=mx(w_ple_gate[i]),
        w_ple_proj=mx(w_ple_proj[i]))


def _rope_tables(pos):
    inv = jnp.power(ROPE_THETA, -jnp.arange(0, QK_ROPE, 2, dtype=F32) / QK_ROPE)
    ang = pos.astype(F32)[:, None] * inv[None, :]
    return jnp.tile(jnp.cos(ang), (1, MLA_HEADS)), jnp.tile(jnp.sin(ang), (1, MLA_HEADS))


def _diag_blocks(cbd):
    return jnp.stack([jnp.swapaxes(cbd[:, 64 * h:64 * h + 64, 64 * h:64 * h + 64], 1, 2)
                      for h in range(MLSTM_HEADS)], axis=1)


def _pad_axis1(a, n, value=0.0):
    pad = [(0, 0)] * a.ndim
    pad[1] = (0, n - a.shape[1])
    return jnp.pad(a, pad, constant_values=value)


def _layer(x, p, lw, cos_tab, sin_tab, b, t, pos0, prefix, c0, n0, m0, attend, final_norm, final):
    qkv, opre, upool, gates, qlat, r, kp, ckv, kr, kt = _inproj(x, lw, cos_tab, sin_tab)
    upool3 = upool.reshape(b, t, D_POOL)
    y_pool = _pool(_pad_axis1(upool3, -(-t // 8) * 8), prefix, lw, pos0)[:, :t].reshape(b * t, D_POOL)
    pool_state = jnp.concatenate([prefix[:, 1:], upool3], axis=1)[:, -POOL_STATE:]

    L = min(MLSTM_CHUNK, -(-t // 8) * 8)
    tp = -(-t // L) * L
    qkv3, opre3, gates3 = qkv.reshape(b, t, 768), opre.reshape(b, t, 256), gates.reshape(b, t, 128)
    if tp != t:
        lane = jnp.arange(128)
        padrow = jnp.where(lane < 4, NEG_BIG, jnp.where(lane < 8, 1e4, 0.0)).astype(F32)
        gates3 = jnp.concatenate([gates3, jnp.broadcast_to(padrow, (b, tp - t, 128))], axis=1)
        qkv3, opre3 = _pad_axis1(qkv3, tp), _pad_axis1(opre3, tp)
    y_ml, c_bd, n_new, m_new = _mlstm(qkv3, opre3, gates3, lw, c0, n0, m0, L)
    y_ml = y_ml[:, :t].reshape(b * t, D_MLSTM)

    y_mla = attend(qlat, r, kp, kt, ckv, kr)
    outs = _ffn(x, y_pool, y_ml, y_mla, p, lw, final_norm, final)
    states = (ckv.reshape(b, t, KV_LORA), kr.reshape(b, t, QK_ROPE), pool_state, _diag_blocks(c_bd),
              n_new.reshape(b, MLSTM_HEADS, MLSTM_DH), m_new[:, 0, :MLSTM_HEADS])
    return outs, states


SAMPLE_PAGES_PER_STEP = 16


def kernel(x_prompt, x_sample, cache_latent, cache_krope, state_pool, state_C, state_n, state_m, page_table, p_prompt, p_sample, norm_mix, w_in, b_igate, b_fgate, pool_w, pool_scale, mlstm_norm, q_norm, w_uq, kv_norm, w_uk, w_uv, w_out, norm_ffn, w_gate, w_up, w_down, ple_norm, w_ple_gate, w_ple_proj, final_norm):
    bp, tp_, _ = x_prompt.shape
    bs, ts, _ = x_sample.shape
    depth = w_in.shape[0]
    past_len = page_table.shape[1] * cache_latent.shape[2]
    cos_p, sin_p = _rope_tables(jnp.arange(tp_))
    cos_s, sin_s = _rope_tables(past_len + jnp.arange(ts))
    cos_s, sin_s = jnp.tile(cos_s, (bs, 1)), jnp.tile(sin_s, (bs, 1))
    fin = final_norm.reshape(1, -1)
    cache_krope_t = jnp.swapaxes(cache_krope, 2, 3)

    xp = x_prompt.reshape(bp * tp_, D_MODEL)
    xs = x_sample.reshape(bs * ts, D_MODEL)
    st_p, st_s = [], []
    yp = ys = None
    for i in range(depth):
        lw = _pack_layer(i, norm_mix, w_in, b_igate, b_fgate, pool_w, pool_scale, mlstm_norm, q_norm, w_uq, kv_norm,
                         w_uk, w_uv, w_out, norm_ffn, w_gate, w_up, w_down, ple_norm, w_ple_gate, w_ple_proj)
        final = i == depth - 1

        def attend_s(qlat, r, kp, kt, ckv, kr):
            ql = qlat.reshape(MLA_HEADS, bs, ts, KV_LORA).transpose(1, 0, 2, 3).reshape(bs, MLA_HEADS * ts, KV_LORA)
            qr = r.reshape(bs, ts, 2, MLA_HEADS, 16).transpose(0, 3, 1, 2, 4).reshape(bs, MLA_HEADS * ts, QK_ROPE)
            cn = _pad_axis1(ckv.reshape(bs, ts, KV_LORA), NEW_PAD)
            kn = _pad_axis1(kr.reshape(bs, ts, QK_ROPE), NEW_PAD)
            o = _attn_sample(ql, qr, cn, kn, cache_latent, cache_krope_t, page_table, i, SAMPLE_PAGES_PER_STEP)
            o = o.reshape(bs, MLA_HEADS, ts, KV_LORA).transpose(1, 0, 2, 3).reshape(MLA_HEADS, bs * ts, KV_LORA)
            return _uvproj(o, lw)

        prefix_s = jnp.concatenate([jnp.zeros((bs, 1, D_POOL), F32), state_pool[i]], axis=1)
        head_eye = jnp.eye(MLSTM_HEADS, dtype=bool)[None, :, None, :, None]
        c0 = jnp.where(head_eye, jnp.swapaxes(state_C[i], 2, 3)[:, :, :, None, :], 0.0)
        c0 = c0.reshape(bs, D_MLSTM, D_MLSTM)
        n0 = state_n[i].reshape(bs, 1, D_MLSTM)
        m0 = _pad_axis1(state_m[i], 128)[:, None, :]
        outs, ss = _layer(xs, p_sample[i].reshape(bs * ts, D_PLE), lw, cos_s, sin_s, bs, ts, past_len,
                          prefix_s, c0, n0, m0, attend_s, fin, final)
        xs = outs[0]
        if final:
            ys = outs[1]

        def attend_p(qlat, r, kp, kt, ckv, kr):
            return _attn_prompt(qlat, r, kp, kt, lw, bp, tp_)

        outs, sp = _layer(xp, p_prompt[i].reshape(bp * tp_, D_PLE), lw, cos_p, sin_p, bp, tp_, 0,
                          jnp.zeros((bp, POOL_HALO, D_POOL), F32), jnp.zeros((bp, D_MLSTM, D_MLSTM), F32),
                          jnp.zeros((bp, 1, D_MLSTM), F32), jnp.zeros((bp, 1, 128), F32), attend_p, fin, final)
        xp = outs[0]
        if final:
            yp = outs[1]
        st_p.append(sp)
        st_s.append(ss)

    stack = lambda sts, k: jnp.stack([s[k] for s in sts])
    return (yp.reshape(bp, tp_, D_MODEL), ys.reshape(bs, ts, D_MODEL),
            stack(st_p, 0), stack(st_p, 1), stack(st_p, 2), stack(st_p, 3), stack(st_p, 4), stack(st_p, 5),
            stack(st_s, 0), stack(st_s, 1), stack(st_s, 2), stack(st_s, 3), stack(st_s, 4), stack(st_s, 5))
```

```python
import functools
import math

import numpy as np
import jax
import jax.numpy as jnp
from jax import lax
from jax.experimental import pallas as pl
from jax.experimental.pallas import tpu as pltpu

D_MODEL = 1024
POOL_WINDOWS = (2, 4, 8, 16)
POOL_GROUP = 64
D_POOL = 256
POOL_STATE = 15
MLSTM_HEADS = 4
MLSTM_DH = 64
D_MLSTM = 256
MLA_HEADS = 8
QK_NOPE = 64
QK_ROPE = 32
V_DIM = 64
D_MLA = 512
Q_LORA = 384
KV_LORA = 256
ROPE_THETA = 10000.0
ATTN_SCALE = (QK_NOPE + QK_ROPE) ** -0.5
SCORE_SCALE_LOG2 = ATTN_SCALE * math.log2(math.e)
D_FF = 2816
D_PLE = 256
EPS = 1e-6

F32 = jnp.float32
MXU_DTYPE = jnp.bfloat16
POOL_HALO = 16
MLSTM_CHUNK = 128
MLSTM_ROWS = 4
MLSTM_ROWS_SHORT = 8
NEG_BIG = -1e30
VMEM_LIMIT = 56 * 1024 * 1024
INPROJ_ROWS = 512
POOL_ROWS = 2048
POOL_BATCH_ROWS = 16

_C_QKV = 0
_C_OPRE = 768
_C_UPOOL = 1024
_C_CQ = 1280
_C_CKV = 1664
_C_GATES = 1920
_C_KRT = 2048
_C_END = 2304


def _dot(a, b):
    return jnp.dot(a, b, preferred_element_type=F32)


def _dot_nt(a, b):
    return lax.dot_general(a, b, (((1,), (1,)), ((), ())), preferred_element_type=F32)


def _rms(x, g):
    return x * lax.rsqrt(jnp.mean(x * x, axis=-1, keepdims=True) + EPS) * g


def _tile(n, pref):
    t = min(n, pref)
    while n % t:
        t //= 2
    return t


def _const_spec(shape):
    nd = len(shape)
    return pl.BlockSpec(shape, lambda *_: (0,) * nd, pipeline_mode=pl.Buffered(1))


def _params(sem):
    return pltpu.CompilerParams(dimension_semantics=sem, vmem_limit_bytes=VMEM_LIMIT)


def _inproj_body(x_ref, nrm_ref, w_ref, qn_ref, kvn_ref, wuq_ref, wukp_ref, cos_ref, sin_ref,
                 qkv_ref, opre_ref, upool_ref, gates_ref, qlat_ref, r_ref, kp_ref, ckv_ref, kr_ref, kt_ref):
    hb = _rms(x_ref[...], nrm_ref[...]).astype(MXU_DTYPE)

    def seg(a, b):
        return _dot(hb, w_ref[:, a:b])

    qkv_ref[:, 0:256] = seg(_C_QKV, _C_QKV + 256).astype(qkv_ref.dtype)
    qkv_ref[:, 256:512] = (seg(_C_QKV + 256, _C_QKV + 512) * (MLSTM_DH ** -0.5)).astype(qkv_ref.dtype)
    qkv_ref[:, 512:768] = seg(_C_QKV + 512, _C_QKV + 768).astype(qkv_ref.dtype)
    opre_ref[...] = seg(_C_OPRE, _C_UPOOL)
    upool_ref[...] = seg(_C_UPOOL, _C_CQ)
    gates_ref[...] = seg(_C_GATES, _C_KRT)

    cos = cos_ref[...]
    sin = sin_ref[...]

    cqn = _rms(seg(_C_CQ, _C_CKV), qn_ref[...]).astype(MXU_DTYPE)
    qf = _dot(cqn, wuq_ref[...])
    r1 = qf[:, 512:640]
    r2 = qf[:, 640:768]
    r_ref[:, 0:128] = (r1 * cos - r2 * sin).astype(r_ref.dtype)
    r_ref[:, 128:256] = (r2 * cos + r1 * sin).astype(r_ref.dtype)
    for p in range(MLA_HEADS // 2):
        ql = _dot(qf[:, 128 * p:128 * p + 128].astype(MXU_DTYPE), wukp_ref[p])
        qlat_ref[2 * p] = ql[:, 0:256].astype(qlat_ref.dtype)
        qlat_ref[2 * p + 1] = ql[:, 256:512].astype(qlat_ref.dtype)

    ckv = _rms(seg(_C_CKV, _C_GATES), kvn_ref[...])
    ckv_ref[...] = ckv
    kp_ref[:, 0:256] = ckv.astype(kp_ref.dtype)
    for c in range(kt_ref.shape[0]):
        kt_ref[c] = ckv[c * kt_ref.shape[2]:(c + 1) * kt_ref.shape[2], :].T.astype(kt_ref.dtype)
    krt = seg(_C_KRT, _C_END)
    k1 = krt[:, 0:128]
    k2 = krt[:, 128:256]
    kt1 = k1 * cos - k2 * sin
    kt2 = k2 * cos + k1 * sin
    kp_ref[:, 256:384] = kt1.astype(kp_ref.dtype)
    kp_ref[:, 384:512] = kt2.astype(kp_ref.dtype)
    kr_ref[...] = jnp.concatenate([kt1[:, 0:16], kt2[:, 0:16]], axis=1)


def _inproj(x, lw, cos_tab, sin_tab):
    n = x.shape[0]
    tm = _tile(n, INPROJ_ROWS)
    tkb = min(ATTN_TQ, tm)
    ntab = cos_tab.shape[0] // tm
    row = lambda w: pl.BlockSpec((tm, w), lambda i: (i, 0))
    tab = pl.BlockSpec((tm, 128), lambda i: (i % ntab, 0))
    out_shape = (
        jax.ShapeDtypeStruct((n, 768), MXU_DTYPE),
        jax.ShapeDtypeStruct((n, 256), F32),
        jax.ShapeDtypeStruct((n, 256), F32),
        jax.ShapeDtypeStruct((n, 128), F32),
        jax.ShapeDtypeStruct((MLA_HEADS, n, 256), MXU_DTYPE),
        jax.ShapeDtypeStruct((n, 256), MXU_DTYPE),
        jax.ShapeDtypeStruct((n, 512), MXU_DTYPE),
        jax.ShapeDtypeStruct((n, 256), F32),
        jax.ShapeDtypeStruct((n, 32), F32),
        jax.ShapeDtypeStruct((n // tkb, KV_LORA, tkb), MXU_DTYPE),
    )
    out_specs = (row(768), row(256), row(256), row(128),
                 pl.BlockSpec((MLA_HEADS, tm, 256), lambda i: (0, i, 0)),
                 row(256), row(512), row(256), row(32),
                 pl.BlockSpec((tm // tkb, KV_LORA, tkb), lambda i: (i, 0, 0)))
    return pl.pallas_call(
        _inproj_body,
        out_shape=out_shape,
        grid=(n // tm,),
        in_specs=[row(D_MODEL), _const_spec((1, D_MODEL)), _const_spec((D_MODEL, _C_END)),
                  _const_spec((1, Q_LORA)), _const_spec((1, KV_LORA)), _const_spec((Q_LORA, 768)),
                  _const_spec((MLA_HEADS // 2, 128, 512)), tab, tab],
        out_specs=out_specs,
        compiler_params=_params(("parallel",)),
        name="inproj",
    )(x, lw["norm_mix"], lw["w_in"], lw["q_norm"], lw["kv_norm"], lw["w_uq"], lw["w_ukp"], cos_tab, sin_tab)


def _pool_body(pre_ref, u_ref, w_ref, sc_ref, y_ref, halo_ref, *, pos0, tc, nb):
    t = pl.program_id(1)

    @pl.when(t == 0)
    def _():
        halo_ref[...] = pre_ref[...]

    lane = lax.broadcasted_iota(jnp.int32, (tc, D_POOL), 1)
    g0, g1, g2 = lane < 64, lane < 128, lane < 192
    wsz = jnp.where(g0, 2, jnp.where(g1, 4, jnp.where(g2, 8, 16)))
    pos = lax.broadcasted_iota(jnp.int32, (tc, D_POOL), 0) + (pos0 + t * tc)
    cnt = jnp.minimum(pos + 1, wsz).astype(F32)
    for r in range(nb):
        u = u_ref[r]
        full = jnp.concatenate([halo_ref[r], u], axis=0)
        a2 = full + pltpu.roll(full, 1, 0)
        a4 = a2 + pltpu.roll(a2, 2, 0)
        a8 = a4 + pltpu.roll(a4, 4, 0)
        a16 = a8 + pltpu.roll(a8, 8, 0)
        win = jnp.where(g0, a2[POOL_HALO:], jnp.where(g1, a4[POOL_HALO:], jnp.where(g2, a8[POOL_HALO:], a16[POOL_HALO:])))
        d = (win / cnt - u).astype(MXU_DTYPE)
        y_ref[r] = (_dot(d, w_ref[...]) * sc_ref[...]).astype(y_ref.dtype)
        halo_ref[r] = full[tc:, :]


def _pool(u, prefix, lw, pos0):
    b, t, _ = u.shape
    tc = _tile(t, POOL_ROWS)
    nb = _tile(b, min(POOL_BATCH_ROWS, max(1, POOL_ROWS // tc)))
    return pl.pallas_call(
        functools.partial(_pool_body, pos0=pos0, tc=tc, nb=nb),
        out_shape=jax.ShapeDtypeStruct((b, t, D_POOL), MXU_DTYPE),
        grid=(b // nb, t // tc),
        in_specs=[pl.BlockSpec((nb, POOL_HALO, D_POOL), lambda i, j: (i, 0, 0)),
                  pl.BlockSpec((nb, tc, D_POOL), lambda i, j: (i, j, 0)),
                  _const_spec((D_POOL, D_POOL)), _const_spec((1, D_POOL))],
        out_specs=pl.BlockSpec((nb, tc, D_POOL), lambda i, j: (i, j, 0)),
        scratch_shapes=[pltpu.VMEM((nb, POOL_HALO, D_POOL), F32)],
        compiler_params=_params(("parallel", "arbitrary")),
        name="pool",
    )(prefix, u, lw["pool_w"], lw["pool_scale"])


def _head_lane_masks(shape, width):
    lane = lax.broadcasted_iota(jnp.int32, shape, len(shape) - 1)
    return [(lane >= width * h) & (lane < width * (h + 1)) for h in range(MLSTM_HEADS)]


def _by_head(masks, vals):
    out = jnp.where(masks[0], vals[0], 0.0)
    for h in range(1, MLSTM_HEADS):
        out = jnp.where(masks[h], vals[h], out)
    return out


def _mlstm_body(qkv_ref, opre_ref, gates_ref, gbias_ref, nrm_ref, c0_ref, n0_ref, m0_ref,
                y_ref, cout_ref, nout_ref, mout_ref, c_ref, n_ref, m_ref, *, L, NB):
    t = pl.program_id(1)

    @pl.when(t == 0)
    def _():
        c_ref[...] = c0_ref[...]
        n_ref[...] = n0_ref[...]
        m_ref[...] = m0_ref[...]

    for r in range(NB):
        _mlstm_row(qkv_ref.at[r], opre_ref.at[r], gates_ref.at[r], gbias_ref, nrm_ref, y_ref.at[r], c_ref.at[r],
                   n_ref.at[r], m_ref.at[r], L=L)

    @pl.when(t == pl.num_programs(1) - 1)
    def _():
        cout_ref[...] = c_ref[...]
        nout_ref[...] = n_ref[...]
        mout_ref[...] = m_ref[...]


def _mlstm_row(qkv_ref, opre_ref, gates_ref, gbias_ref, nrm_ref, y_ref, c_ref, n_ref, m_ref, *, L):
    H, DH = MLSTM_HEADS, MLSTM_DH
    q = qkv_ref[:, 0:256]
    k = qkv_ref[:, 256:512]
    v = qkv_ref[:, 512:768]
    qt = q.astype(F32).T.astype(MXU_DTYPE)
    vt = v.astype(F32).T
    vtb = vt.astype(MXU_DTYPE)
    g = gates_ref[...] + gbias_ref[...]
    gt = g.T
    src_i = lax.broadcasted_iota(jnp.int32, (L, L), 0)
    tgt_i = lax.broadcasted_iota(jnp.int32, (L, L), 1)
    causal = src_i <= tgt_i
    b_cols = jnp.dot((tgt_i <= src_i).astype(F32), jax.nn.log_sigmoid(g), preferred_element_type=F32,
                     precision=lax.Precision.HIGHEST)
    b_rows = jnp.dot(jax.nn.log_sigmoid(gt), causal.astype(F32), preferred_element_type=F32,
                     precision=lax.Precision.HIGHEST)

    m_prev_all = m_ref[...]
    masks = _head_lane_masks((L, D_MLSTM), DH)
    zero_k = jnp.zeros_like(k)
    ct = c_ref[...]
    inter_num = _dot(ct.astype(MXU_DTYPE), qt)
    head_row = lax.broadcasted_iota(jnp.int32, (8, D_MLSTM), 0)
    head_lane = lax.broadcasted_iota(jnp.int32, (8, D_MLSTM), 1) // DH
    n8 = jnp.where(head_row == head_lane, jnp.broadcast_to(n_ref[...], (8, D_MLSTM)), 0.0)
    qn = _dot(n8.astype(MXU_DTYPE), qt)

    hn_parts, vw_parts, wk_rows, dec_11, mnew_11 = [], [], [], [], []
    for h in range(H):
        rs = slice(DH * h, DH * (h + 1))
        b_row = b_rows[4 + h:5 + h, :]
        ig_row = gt[h:h + 1, :]
        m_prev = m_prev_all[:, h:h + 1]
        src_col = b_cols[:, 4 + h:5 + h] - g[:, h:h + 1]
        logw = jnp.where(causal, b_row - src_col, -jnp.inf)
        inter = b_row + m_prev
        m_t = jnp.maximum(inter, jnp.max(logw, axis=0, keepdims=True))
        w = jnp.exp(logw - m_t)
        a = jnp.exp(inter - m_t)
        st = _dot_nt(jnp.where(masks[h], k, zero_k), q) * w
        num = _dot(vtb[rs, :], st.astype(MXU_DTYPE)) + a * inter_num[rs, :]
        den = jnp.sum(st, axis=0, keepdims=True) + a * qn[h:h + 1, :]
        hm = num / jnp.maximum(jnp.abs(den), jnp.exp(-m_t))
        ms = jnp.mean(hm * hm, axis=0, keepdims=True)
        hn_parts.append(hm * lax.rsqrt(ms + EPS))
        m_new = m_t[:, L - 1:L]
        b_last = b_row[:, L - 1:L]
        wk = jnp.exp(b_last - b_row + ig_row - m_new)
        mnew_11.append(m_new)
        dec_11.append(jnp.exp(b_last + m_prev - m_new))
        wk_rows.append(wk)
        vw_parts.append(vt[rs, :] * wk)

    hn = jnp.concatenate(hn_parts, axis=0) * nrm_ref[:, 0:L]
    y_ref[...] = (hn * jax.nn.sigmoid(opre_ref[...].T)).T.astype(y_ref.dtype)

    c_add = _dot(jnp.concatenate(vw_parts, axis=0).astype(MXU_DTYPE), k)
    wk8 = jnp.concatenate(wk_rows + [jnp.zeros((8 - H, L), F32)], axis=0).astype(MXU_DTYPE)
    n_add8 = _dot(wk8, k)
    masks_sq = _head_lane_masks((D_MLSTM, D_MLSTM), MLSTM_DH)
    rows_sq = lax.broadcasted_iota(jnp.int32, (D_MLSTM, D_MLSTM), 0)
    diag = jnp.zeros((D_MLSTM, D_MLSTM), jnp.bool_)
    for h in range(H):
        diag = diag | (masks_sq[h] & (rows_sq >= MLSTM_DH * h) & (rows_sq < MLSTM_DH * (h + 1)))
    lane1 = _head_lane_masks((1, D_MLSTM), MLSTM_DH)
    dec_lane = _by_head(lane1, dec_11)
    c_new = dec_lane * ct + jnp.where(diag, c_add, 0.0)
    n_new = dec_lane * n_ref[...] + _by_head(lane1, [n_add8[h:h + 1, :] for h in range(H)])
    lane_m = lax.broadcasted_iota(jnp.int32, (1, 128), 1)
    m_new_all = m_prev_all
    for h in range(H):
        m_new_all = jnp.where(lane_m == h, mnew_11[h], m_new_all)
    c_ref[...] = c_new
    n_ref[...] = n_new
    m_ref[...] = m_new_all


def _mlstm(qkv, opre, gates, lw, c0, n0, m0, L):
    b, t, _ = qkv.shape
    NB = _tile(b, MLSTM_ROWS if L == MLSTM_CHUNK else MLSTM_ROWS_SHORT)
    seq = lambda w: pl.BlockSpec((NB, L, w), lambda i, j: (i, j, 0))
    st = lambda r, w: pl.BlockSpec((NB, r, w), lambda i, j: (i, 0, 0))
    return pl.pallas_call(
        functools.partial(_mlstm_body, L=L, NB=NB),
        out_shape=(jax.ShapeDtypeStruct((b, t, D_MLSTM), MXU_DTYPE),
                   jax.ShapeDtypeStruct((b, D_MLSTM, D_MLSTM), F32),
                   jax.ShapeDtypeStruct((b, 1, D_MLSTM), F32),
                   jax.ShapeDtypeStruct((b, 1, 128), F32)),
        grid=(b // NB, t // L),
        in_specs=[seq(768), seq(256), seq(128), _const_spec((1, 128)), _const_spec((D_MLSTM, MLSTM_CHUNK)),
                  st(D_MLSTM, D_MLSTM), st(1, D_MLSTM), st(1, 128)],
        out_specs=(seq(256), st(D_MLSTM, D_MLSTM), st(1, D_MLSTM), st(1, 128)),
        scratch_shapes=[pltpu.VMEM((NB, D_MLSTM, D_MLSTM), F32), pltpu.VMEM((NB, 1, D_MLSTM), F32),
                        pltpu.VMEM((NB, 1, 128), F32)],
        compiler_params=_params(("parallel", "arbitrary")),
        name="mlstm",
    )(qkv, opre, gates, lw["gbias"], lw["mlstm_norm_t"], c0, n0, m0)


def _head_proj(o_heads, wuvp_ref, y_ref):
    for p in range(MLA_HEADS // 2):
        op = jnp.concatenate([o_heads(2 * p), o_heads(2 * p + 1)], axis=1).astype(MXU_DTYPE)
        y_ref[:, 128 * p:128 * p + 128] = _dot(op, wuvp_ref[p]).astype(y_ref.dtype)


ATTN_TQ = 256


def _attn_body(qlat_ref, r_ref, k_ref, kt_ref, wuvt_ref, y_ref, q2_ref, m_ref, l_ref, acc_ref, *, tq):
    i = pl.program_id(1)
    r = r_ref[...]
    lane = lax.broadcasted_iota(jnp.int32, (tq, 256), 1) & 127
    zero_r = jnp.zeros_like(r)
    for h in range(MLA_HEADS):
        q2_ref[h * tq:(h + 1) * tq, 0:256] = qlat_ref[h]
        q2_ref[h * tq:(h + 1) * tq, 256:512] = jnp.where((lane >= 16 * h) & (lane < 16 * h + 16), r, zero_r)
    m_ref[...] = jnp.full(m_ref.shape, -jnp.inf, F32)
    l_ref[...] = jnp.zeros(l_ref.shape, F32)
    acc_ref[...] = jnp.zeros(acc_ref.shape, F32)
    nq = MLA_HEADS * tq

    def block(j, nblk, masked):
        kb = k_ref[pl.ds(pl.multiple_of(j * tq, tq), nblk * tq), :]
        vt = kt_ref[j] if nblk == 1 else jnp.concatenate([kt_ref[j + c] for c in range(nblk)], axis=1)
        st = _dot_nt(kb, q2_ref[...]) * SCORE_SCALE_LOG2
        if masked:
            kpos = lax.broadcasted_iota(jnp.int32, (tq, nq), 0)
            qpos = lax.broadcasted_iota(jnp.int32, (tq, nq), 1) & (tq - 1)
            st = jnp.where(kpos <= qpos, st, -jnp.inf)
        m_old = m_ref[...]
        m_new = jnp.maximum(m_old, jnp.max(st, axis=0, keepdims=True))
        alpha = jnp.exp2(m_old - m_new)
        p = jnp.exp2(st - m_new)
        l_ref[...] = alpha * l_ref[...] + jnp.sum(p, axis=0, keepdims=True)
        acc_ref[...] = alpha * acc_ref[...] + _dot(vt, p.astype(MXU_DTYPE))
        m_ref[...] = m_new

    def body(j, carry):
        block(2 * j, 2, False)
        return carry

    lax.fori_loop(0, i // 2, body, 0)

    @pl.when(i % 2 == 1)
    def _():
        block(i - 1, 1, False)

    block(i, 1, True)

    o = (acc_ref[...] / l_ref[...]).astype(MXU_DTYPE)
    yt = [_dot(wuvt_ref[h], o[:, h * tq:(h + 1) * tq]) for h in range(MLA_HEADS)]
    y_ref[...] = jnp.concatenate(yt, axis=0).T.astype(y_ref.dtype)


def _attn_prompt(qlat, r, kp, kt, lw, b, t):
    tq = ATTN_TQ
    nq = t // tq
    return pl.pallas_call(
        functools.partial(_attn_body, tq=tq),
        out_shape=jax.ShapeDtypeStruct((b * t, D_MLA), MXU_DTYPE),
        grid=(b, nq),
        in_specs=[pl.BlockSpec((MLA_HEADS, tq, 256), lambda i, j: (0, i * nq + j, 0)),
                  pl.BlockSpec((tq, 256), lambda i, j: (i * nq + j, 0)),
                  pl.BlockSpec((t, 512), lambda i, j: (i, 0)),
                  pl.BlockSpec((nq, KV_LORA, tq), lambda i, j: (i, 0, 0)),
                  _const_spec((MLA_HEADS, V_DIM, KV_LORA))],
        out_specs=pl.BlockSpec((tq, D_MLA), lambda i, j: (i * nq + j, 0)),
        scratch_shapes=[pltpu.VMEM((MLA_HEADS * tq, 512), MXU_DTYPE), pltpu.VMEM((1, MLA_HEADS * tq), F32),
                        pltpu.VMEM((1, MLA_HEADS * tq), F32), pltpu.VMEM((KV_LORA, MLA_HEADS * tq), F32)],
        compiler_params=_params(("parallel", "arbitrary")),
        name="attn_prompt",
    )(qlat, r, kp, kt, lw["w_uvt"])


NEW_PAD = 16
SAMPLE_SLOTS = 4


def _attn_sample_body(pt_ref, ql_ref, qr_ref, cn_ref, kn_ref, lat_hbm, krt_hbm, o_ref,
                      lat_buf, kr_buf, sem_lat, sem_kr, *, G, T, layer, n_pages):
    b = pl.program_id(0)
    n_rows = pl.num_programs(0)
    n_groups = n_pages // G

    def group_copies(row, grp, slot):
        out = []
        for g in range(G):
            page = pt_ref[row * n_pages + grp * G + g]
            out.append(pltpu.make_async_copy(lat_hbm.at[layer, page], lat_buf.at[slot, g], sem_lat.at[slot]))
            out.append(pltpu.make_async_copy(krt_hbm.at[layer, page], kr_buf.at[slot, g], sem_kr.at[slot]))
        return out

    def start(row, grp, slot):
        for c in group_copies(row, grp, slot):
            c.start()

    def wait(row, grp, slot):
        for c in group_copies(row, grp, slot):
            c.wait()

    NS = SAMPLE_SLOTS

    @pl.when(b == 0)
    def _():
        for g in range(NS - 1):
            start(0, g, g)

    def start_ahead(g):
        nxt = g + NS - 1
        if nxt < n_groups:
            start(b, nxt, nxt % NS)
        else:
            @pl.when(b + 1 < n_rows)
            def _():
                start(b + 1, nxt - n_groups, nxt % NS)

    ql = ql_ref[...]
    qr = qr_ref[...]

    def values(slot):
        return [lat_buf[slot, g].astype(MXU_DTYPE) for g in range(G)]

    def scores(slot):
        cls = values(slot)
        return jnp.concatenate([_dot_nt(ql, cls[g]) + _dot(qr, kr_buf[slot, g].astype(MXU_DTYPE))
                                for g in range(G)], axis=1) * ATTN_SCALE

    def update(stats, s, vals):
        m_old, l_old, acc = stats
        m_new = jnp.maximum(m_old, jnp.max(s, axis=1, keepdims=True))
        alpha = jnp.exp(m_old - m_new)
        pf = jnp.exp(s - m_new)
        p = pf.astype(MXU_DTYPE)
        acc = alpha * acc
        off = 0
        for vb in vals:
            acc = acc + _dot(p[:, off:off + vb.shape[0]], vb)
            off += vb.shape[0]
        return m_new, alpha * l_old + jnp.sum(pf, axis=1, keepdims=True), acc

    rows = MLA_HEADS * T
    stats = (jnp.full((rows, 1), -jnp.inf, F32), jnp.zeros((rows, 1), F32), jnp.zeros((rows, KV_LORA), F32))
    wait(b, 0, 0)
    s_cur = scores(0)
    for grp in range(n_groups):
        if grp + 1 < n_groups:
            wait(b, grp + 1, (grp + 1) % NS)
        start_ahead(grp)
        if grp + 1 < n_groups:
            s_next = scores((grp + 1) % NS)
        stats = update(stats, s_cur, values(grp % NS))
        if grp + 1 < n_groups:
            s_cur = s_next

    cn = cn_ref[...].astype(MXU_DTYPE)
    kn = kn_ref[...].astype(MXU_DTYPE)
    sn = (_dot_nt(ql, cn) + _dot_nt(qr, kn)) * ATTN_SCALE
    qpos = lax.broadcasted_iota(jnp.int32, sn.shape, 0) % T
    kpos = lax.broadcasted_iota(jnp.int32, sn.shape, 1)
    _, l_fin, acc_fin = update(stats, jnp.where(kpos <= qpos, sn, -jnp.inf), [cn])
    o_ref[...] = acc_fin / l_fin


def _attn_sample(ql, qr, cn, kn, cache_latent, cache_krope_t, page_table, layer, G):
    db, rows, _ = ql.shape
    T = rows // MLA_HEADS
    n_pages = page_table.shape[1]
    page = cache_latent.shape[2]
    G = _tile(n_pages, G)
    NS = SAMPLE_SLOTS
    assert (n_pages // G) % NS == 0, "page groups per row must be a multiple of the number of VMEM slots"
    per_b = lambda r, w: pl.BlockSpec((None, r, w), lambda i, pt: (i, 0, 0))
    hbm = pl.BlockSpec(memory_space=pl.ANY)
    grid_spec = pltpu.PrefetchScalarGridSpec(
        num_scalar_prefetch=1,
        grid=(db,),
        in_specs=[per_b(rows, KV_LORA), per_b(rows, QK_ROPE), per_b(NEW_PAD, KV_LORA), per_b(NEW_PAD, QK_ROPE), hbm, hbm],
        out_specs=pl.BlockSpec((None, rows, KV_LORA), lambda i, pt: (i, 0, 0)),
        scratch_shapes=[pltpu.VMEM((NS, G, page, KV_LORA), F32), pltpu.VMEM((NS, G, QK_ROPE, page), F32),
                        pltpu.SemaphoreType.DMA((NS,)), pltpu.SemaphoreType.DMA((NS,))],
    )
    return pl.pallas_call(
        functools.partial(_attn_sample_body, G=G, T=T, layer=layer, n_pages=n_pages),
        out_shape=jax.ShapeDtypeStruct((db, rows, KV_LORA), F32),
        grid_spec=grid_spec,
        compiler_params=_params(("arbitrary",)),
        name="attn_sample",
    )(page_table.reshape(-1), ql, qr, cn, kn, cache_latent, cache_krope_t)


def _uvproj_body(o_ref, wuvp_ref, y_ref):
    _head_proj(lambda h: o_ref[h], wuvp_ref, y_ref)


def _uvproj(o, lw):
    n = o.shape[1]
    return pl.pallas_call(
        _uvproj_body,
        out_shape=jax.ShapeDtypeStruct((n, D_MLA), MXU_DTYPE),
        grid=(1,),
        in_specs=[pl.BlockSpec((MLA_HEADS, n, 256), lambda i: (0, 0, 0)), _const_spec((MLA_HEADS // 2, 512, 128))],
        out_specs=pl.BlockSpec((n, D_MLA), lambda i: (0, 0)),
        compiler_params=_params(("arbitrary",)),
        name="uvproj",
    )(o, lw["w_uvp"])


FFN_CHUNK = 256


def _ffn_body(x_ref, yp_ref, ym_ref, ya_ref, p_ref, wout_ref, nf_ref, wg_ref, wu_ref, wd_ref, pn_ref, wpg_ref,
              wpp_ref, fn_ref, xo_ref, *maybe_yo, final):
    mix = (_dot(yp_ref[...], wout_ref[0:256, :]) + _dot(ym_ref[...], wout_ref[256:512, :])
           + _dot(ya_ref[...], wout_ref[512:1024, :]))
    x1 = x_ref[...] + mix
    hf = _rms(x1, nf_ref[...]).astype(MXU_DTYPE)
    acc = jnp.zeros_like(x1)
    for c in range(D_FF // FFN_CHUNK):
        sl = slice(c * FFN_CHUNK, (c + 1) * FFN_CHUNK)
        gate = _dot(hf, wg_ref[:, sl])
        up = _dot(hf, wu_ref[:, sl])
        acc = acc + _dot((gate * jax.nn.sigmoid(gate) * up).astype(MXU_DTYPE), wd_ref[sl, :])
    x2 = x1 + acc
    pg = jax.nn.sigmoid(_dot(_rms(x2, pn_ref[...]).astype(MXU_DTYPE), wpg_ref[...]))
    x3 = x2 + pg * _dot(p_ref[...].astype(MXU_DTYPE), wpp_ref[...])
    xo_ref[...] = x3
    if final:
        maybe_yo[0][...] = _rms(x3, fn_ref[...])


def _ffn(x, yp, ym, ya, p, layer, lw, final_norm, final):
    n = x.shape[0]
    tm = _tile(n, 512)
    row = lambda w: pl.BlockSpec((tm, w), lambda i: (i, 0))
    p_spec = pl.BlockSpec((None, tm, D_PLE), lambda i: (layer, i, 0))
    out_shape = [jax.ShapeDtypeStruct((n, D_MODEL), F32)]
    out_specs = [row(D_MODEL)]
    if final:
        out_shape.append(jax.ShapeDtypeStruct((n, D_MODEL), F32))
        out_specs.append(row(D_MODEL))
    return pl.pallas_call(
        functools.partial(_ffn_body, final=final),
        out_shape=tuple(out_shape),
        grid=(n // tm,),
        in_specs=[row(D_MODEL), row(D_POOL), row(D_MLSTM), row(D_MLA), p_spec,
                  _const_spec((D_MODEL, D_MODEL)), _const_spec((1, D_MODEL)),
                  _const_spec((D_MODEL, D_FF)), _const_spec((D_MODEL, D_FF)), _const_spec((D_FF, D_MODEL)),
                  _const_spec((1, D_MODEL)), _const_spec((D_MODEL, D_MODEL)), _const_spec((D_PLE, D_MODEL)),
                  _const_spec((1, D_MODEL))],
        out_specs=tuple(out_specs),
        compiler_params=_params(("parallel",)),
        name="ffn",
    )(x, yp, ym, ya, p, lw["w_out"], lw["norm_ffn"], lw["w_gate"], lw["w_up"], lw["w_down"], lw["ple_norm"],
      lw["w_ple_gate"], lw["w_ple_proj"], final_norm)


def _block_diag(blocks):
    n = len(blocks)
    r, c = blocks[0].shape
    out = jnp.zeros((n * r, n * c), blocks[0].dtype)
    for i, blk in enumerate(blocks):
        out = out.at[i * r:(i + 1) * r, i * c:(i + 1) * c].set(blk)
    return out


def _pack_layer(i, norm_mix, w_in, b_igate, b_fgate, pool_w, pool_scale, mlstm_norm, q_norm, w_uq, kv_norm, w_uk,
                w_uv, w_out, norm_ffn, w_gate, w_up, w_down, ple_norm, w_ple_gate, w_ple_proj):
    wi = w_in[i]
    kr = wi[:, 1928:1960]
    packed = jnp.concatenate([
        wi[:, 256:1024], wi[:, 1024:1280], wi[:, 0:256], wi[:, 1288:1672], wi[:, 1672:1928],
        wi[:, 1280:1288], jnp.zeros((D_MODEL, 120), wi.dtype),
        jnp.tile(kr[:, 0:16], (1, MLA_HEADS)), jnp.tile(kr[:, 16:32], (1, MLA_HEADS))], axis=1)
    wq = w_uq[i]
    w_uq_p = jnp.concatenate([wq[:, :, 0:64].reshape(Q_LORA, 512), wq[:, :, 64:80].reshape(Q_LORA, 128),
                              wq[:, :, 80:96].reshape(Q_LORA, 128)], axis=1)
    wk = w_uk[i]
    w_ukp = jnp.stack([_block_diag([wk[:, 2 * p, :].T, wk[:, 2 * p + 1, :].T]) for p in range(MLA_HEADS // 2)])
    wv = w_uv[i]
    w_uvp = jnp.stack([_block_diag([wv[:, 2 * p, :], wv[:, 2 * p + 1, :]]) for p in range(MLA_HEADS // 2)])
    w_uvt = jnp.transpose(wv, (1, 2, 0))
    gbias = jnp.concatenate([b_igate[i], b_fgate[i], jnp.zeros((120,), F32)]).reshape(1, 128)
    mx = lambda a: a.astype(MXU_DTYPE)
    return dict(
        norm_mix=norm_mix[i].reshape(1, -1), w_in=mx(packed), q_norm=q_norm[i].reshape(1, -1),
        kv_norm=kv_norm[i].reshape(1, -1), w_uq=mx(w_uq_p), w_ukp=mx(w_ukp), w_uvp=mx(w_uvp), w_uvt=mx(w_uvt),
        pool_w=mx(_block_diag([pool_w[i, g] for g in range(4)])), pool_scale=pool_scale[i].reshape(1, -1),
        gbias=gbias, mlstm_norm_t=jnp.broadcast_to(mlstm_norm[i][:, None], (D_MLSTM, MLSTM_CHUNK)),
        w_out=mx(w_out[i]), norm_ffn=norm_ffn[i].reshape(1, -1), w_gate=mx(w_gate[i]), w_up=mx(w_up[i]),
        w_down=mx(w_down[i]), ple_norm=ple_norm[i].reshape(1, -1), w_ple_gate=mx(w_ple_gate[i]),
        w_ple_proj=mx(w_ple_proj[i]))


def _rope_tables(pos):
    inv = jnp.power(ROPE_THETA, -jnp.arange(0, QK_ROPE, 2, dtype=F32) / QK_ROPE)
    ang = pos.astype(F32)[:, None] * inv[None, :]
    return jnp.tile(jnp.cos(ang), (1, MLA_HEADS)), jnp.tile(jnp.sin(ang), (1, MLA_HEADS))


def _diag_blocks(cbd):
    return jnp.stack([jnp.swapaxes(cbd[:, 64 * h:64 * h + 64, 64 * h:64 * h + 64], 1, 2)
                      for h in range(MLSTM_HEADS)], axis=1)


def _pad_axis1(a, n, value=0.0):
    pad = [(0, 0)] * a.ndim
    pad[1] = (0, n - a.shape[1])
    return jnp.pad(a, pad, constant_values=value)


def _layer(x, p, layer, lw, cos_tab, sin_tab, b, t, pos0, prefix, c0, n0, m0, attend, final_norm, final):
    qkv, opre, upool, gates, qlat, r, kp, ckv, kr, kt = _inproj(x, lw, cos_tab, sin_tab)
    upool3 = upool.reshape(b, t, D_POOL)
    y_pool = _pool(_pad_axis1(upool3, -(-t // 8) * 8), prefix, lw, pos0)[:, :t].reshape(b * t, D_POOL)
    pool_state = jnp.concatenate([prefix[:, 1:], upool3], axis=1)[:, -POOL_STATE:]

    L = min(MLSTM_CHUNK, -(-t // 8) * 8)
    tp = -(-t // L) * L
    qkv3, opre3, gates3 = qkv.reshape(b, t, 768), opre.reshape(b, t, 256), gates.reshape(b, t, 128)
    if tp != t:
        lane = jnp.arange(128)
        padrow = jnp.where(lane < 4, NEG_BIG, jnp.where(lane < 8, 1e4, 0.0)).astype(F32)
        gates3 = jnp.concatenate([gates3, jnp.broadcast_to(padrow, (b, tp - t, 128))], axis=1)
        qkv3, opre3 = _pad_axis1(qkv3, tp), _pad_axis1(opre3, tp)
    y_ml, c_bd, n_new, m_new = _mlstm(qkv3, opre3, gates3, lw, c0, n0, m0, L)
    y_ml = y_ml[:, :t].reshape(b * t, D_MLSTM)

    y_mla = attend(qlat, r, kp, kt, ckv, kr)
    outs = _ffn(x, y_pool, y_ml, y_mla, p, layer, lw, final_norm, final)
    states = (ckv.reshape(b, t, KV_LORA), kr.reshape(b, t, QK_ROPE), pool_state, _diag_blocks(c_bd),
              n_new.reshape(b, MLSTM_HEADS, MLSTM_DH), m_new[:, 0, :MLSTM_HEADS])
    return outs, states


SAMPLE_PAGES_PER_STEP = 16


def kernel(x_prompt, x_sample, cache_latent, cache_krope, state_pool, state_C, state_n, state_m, page_table, p_prompt, p_sample, norm_mix, w_in, b_igate, b_fgate, pool_w, pool_scale, mlstm_norm, q_norm, w_uq, kv_norm, w_uk, w_uv, w_out, norm_ffn, w_gate, w_up, w_down, ple_norm, w_ple_gate, w_ple_proj, final_norm):
    bp, tp_, _ = x_prompt.shape
    bs, ts, _ = x_sample.shape
    depth = w_in.shape[0]
    past_len = page_table.shape[1] * cache_latent.shape[2]
    cos_p, sin_p = _rope_tables(jnp.arange(tp_))
    cos_s, sin_s = _rope_tables(past_len + jnp.arange(ts))
    cos_s, sin_s = jnp.tile(cos_s, (bs, 1)), jnp.tile(sin_s, (bs, 1))
    fin = final_norm.reshape(1, -1)
    cache_krope_t = jnp.swapaxes(cache_krope, 2, 3)

    xp = x_prompt.reshape(bp * tp_, D_MODEL)
    xs = x_sample.reshape(bs * ts, D_MODEL)
    st_p, st_s = [], []
    yp = ys = None
    for i in range(depth):
        lw = _pack_layer(i, norm_mix, w_in, b_igate, b_fgate, pool_w, pool_scale, mlstm_norm, q_norm, w_uq, kv_norm,
                         w_uk, w_uv, w_out, norm_ffn, w_gate, w_up, w_down, ple_norm, w_ple_gate, w_ple_proj)
        final = i == depth - 1

        def attend_s(qlat, r, kp, kt, ckv, kr):
            ql = qlat.reshape(MLA_HEADS, bs, ts, KV_LORA).transpose(1, 0, 2, 3).reshape(bs, MLA_HEADS * ts, KV_LORA)
            qr = r.reshape(bs, ts, 2, MLA_HEADS, 16).transpose(0, 3, 1, 2, 4).reshape(bs, MLA_HEADS * ts, QK_ROPE)
            cn = _pad_axis1(ckv.reshape(bs, ts, KV_LORA), NEW_PAD)
            kn = _pad_axis1(kr.reshape(bs, ts, QK_ROPE), NEW_PAD)
            o = _attn_sample(ql, qr, cn, kn, cache_latent, cache_krope_t, page_table, i, SAMPLE_PAGES_PER_STEP)
            o = o.reshape(bs, MLA_HEADS, ts, KV_LORA).transpose(1, 0, 2, 3).reshape(MLA_HEADS, bs * ts, KV_LORA)
            return _uvproj(o, lw)

        prefix_s = jnp.concatenate([jnp.zeros((bs, 1, D_POOL), F32), state_pool[i]], axis=1)
        c0 = jnp.concatenate([jnp.pad(jnp.swapaxes(state_C[i][:, h], 1, 2), ((0, 0), (0, 0), (64 * h, 192 - 64 * h)))
                              for h in range(MLSTM_HEADS)], axis=1)
        n0 = state_n[i].reshape(bs, 1, D_MLSTM)
        m0 = _pad_axis1(state_m[i], 128)[:, None, :]
        outs, ss = _layer(xs, p_sample.reshape(depth, bs * ts, D_PLE), i, lw, cos_s, sin_s, bs, ts, past_len,
                          prefix_s, c0, n0, m0, attend_s, fin, final)
        xs = outs[0]
        if final:
            ys = outs[1]

        def attend_p(qlat, r, kp, kt, ckv, kr):
            return _attn_prompt(qlat, r, kp, kt, lw, bp, tp_)

        outs, sp = _layer(xp, p_prompt.reshape(depth, bp * tp_, D_PLE), i, lw, cos_p, sin_p, bp, tp_, 0,
                          jnp.zeros((bp, POOL_HALO, D_POOL), F32), jnp.zeros((bp, D_MLSTM, D_MLSTM), F32),
                          jnp.zeros((bp, 1, D_MLSTM), F32), jnp.zeros((bp, 1, 128), F32), attend_p, fin, final)
        xp = outs[0]
        if final:
            yp = outs[1]
        st_p.append(sp)
        st_s.append(ss)

    stack = lambda sts, k: jnp.stack([s[k] for s in sts])
    return (yp.reshape(bp, tp_, D_MODEL), ys.reshape(bs, ts, D_MODEL),
            stack(st_p, 0), stack(st_p, 1), stack(st_p, 2), stack(st_p, 3), stack(st_p, 4), stack(st_p, 5),
            stack(st_s, 0), stack(st_s, 1), stack(st_s, 2), stack(st_s, 3), stack(st_s, 4), stack(st_s, 5))
```

```python
import functools
import math

import jax
import jax.numpy as jnp
from jax import lax
from jax.experimental import pallas as pl
from jax.experimental.pallas import tpu as pltpu

D_MODEL = 1024
POOL_WINDOWS = (2, 4, 8, 16)
POOL_GROUP = 64
D_POOL = 256
POOL_STATE = 15
MLSTM_HEADS = 4
MLSTM_DH = 64
D_MLSTM = 256
MLA_HEADS = 8
QK_NOPE = 64
QK_ROPE = 32
V_DIM = 64
D_MLA = 512
Q_LORA = 384
KV_LORA = 256
ROPE_THETA = 10000.0
ATTN_SCALE = (QK_NOPE + QK_ROPE) ** -0.5
SCORE_SCALE_LOG2 = ATTN_SCALE * math.log2(math.e)
D_FF = 2816
D_PLE = 256
EPS = 1e-6

F32 = jnp.float32
MXU_DTYPE = jnp.bfloat16
POOL_HALO = 16
MLSTM_CHUNK = 128
MLSTM_ROWS = 4
MLSTM_ROWS_SHORT = 8
NEG_BIG = -1e30
VMEM_LIMIT = 56 * 1024 * 1024
INPROJ_ROWS = 512
POOL_ROWS = 2048
POOL_BATCH_ROWS = 16

_C_QKV = 0
_C_OPRE = 768
_C_UPOOL = 1024
_C_CQ = 1280
_C_CKV = 1664
_C_GATES = 1920
_C_KRT = 2048
_C_END = 2304


def _dot(a, b):
    return jnp.dot(a, b, preferred_element_type=F32)


def _dot_nt(a, b):
    return lax.dot_general(a, b, (((1,), (1,)), ((), ())), preferred_element_type=F32)


def _rms(x, g):
    return x * lax.rsqrt(jnp.mean(x * x, axis=-1, keepdims=True) + EPS) * g


def _tile(n, pref):
    t = min(n, pref)
    while n % t:
        t //= 2
    return t


def _const_spec(shape):
    nd = len(shape)
    return pl.BlockSpec(shape, lambda *_: (0,) * nd, pipeline_mode=pl.Buffered(1))


def _params(sem):
    return pltpu.CompilerParams(dimension_semantics=sem, vmem_limit_bytes=VMEM_LIMIT)


def _inproj_body(x_ref, nrm_ref, w_ref, qn_ref, kvn_ref, wuq_ref, wukp_ref, cos_ref, sin_ref,
                 qkv_ref, opre_ref, upool_ref, gates_ref, qlat_ref, r_ref, kp_ref, ckv_ref, kr_ref, kt_ref):
    hb = _rms(x_ref[...], nrm_ref[...]).astype(MXU_DTYPE)

    def seg(a, b):
        return _dot(hb, w_ref[:, a:b])

    qkv_ref[:, 0:256] = seg(_C_QKV, _C_QKV + 256).astype(qkv_ref.dtype)
    qkv_ref[:, 256:512] = (seg(_C_QKV + 256, _C_QKV + 512) * (MLSTM_DH ** -0.5)).astype(qkv_ref.dtype)
    qkv_ref[:, 512:768] = seg(_C_QKV + 512, _C_QKV + 768).astype(qkv_ref.dtype)
    opre_ref[...] = seg(_C_OPRE, _C_UPOOL)
    upool_ref[...] = seg(_C_UPOOL, _C_CQ)
    gates_ref[...] = seg(_C_GATES, _C_KRT)

    cos = cos_ref[...]
    sin = sin_ref[...]

    cqn = _rms(seg(_C_CQ, _C_CKV), qn_ref[...]).astype(MXU_DTYPE)
    qf = _dot(cqn, wuq_ref[...])
    r1 = qf[:, 512:640]
    r2 = qf[:, 640:768]
    r_ref[:, 0:128] = (r1 * cos - r2 * sin).astype(r_ref.dtype)
    r_ref[:, 128:256] = (r2 * cos + r1 * sin).astype(r_ref.dtype)
    for p in range(MLA_HEADS // 2):
        ql = _dot(qf[:, 128 * p:128 * p + 128].astype(MXU_DTYPE), wukp_ref[p])
        qlat_ref[2 * p] = ql[:, 0:256].astype(qlat_ref.dtype)
        qlat_ref[2 * p + 1] = ql[:, 256:512].astype(qlat_ref.dtype)

    ckv = _rms(seg(_C_CKV, _C_GATES), kvn_ref[...])
    ckv_ref[...] = ckv
    kp_ref[:, 0:256] = ckv.astype(kp_ref.dtype)
    for c in range(kt_ref.shape[0]):
        kt_ref[c] = ckv[c * kt_ref.shape[2]:(c + 1) * kt_ref.shape[2], :].T.astype(kt_ref.dtype)
    krt = seg(_C_KRT, _C_END)
    k1 = krt[:, 0:128]
    k2 = krt[:, 128:256]
    kt1 = k1 * cos - k2 * sin
    kt2 = k2 * cos + k1 * sin
    kp_ref[:, 256:384] = kt1.astype(kp_ref.dtype)
    kp_ref[:, 384:512] = kt2.astype(kp_ref.dtype)
    kr_ref[...] = jnp.concatenate([kt1[:, 0:16], kt2[:, 0:16]], axis=1)


def _inproj(x, lw, cos_tab, sin_tab):
    n = x.shape[0]
    tm = _tile(n, INPROJ_ROWS)
    tkb = min(ATTN_TQ, tm)
    ntab = cos_tab.shape[0] // tm
    row = lambda w: pl.BlockSpec((tm, w), lambda i: (i, 0))
    tab = pl.BlockSpec((tm, 128), lambda i: (i % ntab, 0))
    out_shape = (
        jax.ShapeDtypeStruct((n, 768), MXU_DTYPE),
        jax.ShapeDtypeStruct((n, 256), F32),
        jax.ShapeDtypeStruct((n, 256), F32),
        jax.ShapeDtypeStruct((n, 128), F32),
        jax.ShapeDtypeStruct((MLA_HEADS, n, 256), MXU_DTYPE),
        jax.ShapeDtypeStruct((n, 256), MXU_DTYPE),
        jax.ShapeDtypeStruct((n, 512), MXU_DTYPE),
        jax.ShapeDtypeStruct((n, 256), F32),
        jax.ShapeDtypeStruct((n, 32), F32),
        jax.ShapeDtypeStruct((n // tkb, KV_LORA, tkb), MXU_DTYPE),
    )
    out_specs = (row(768), row(256), row(256), row(128),
                 pl.BlockSpec((MLA_HEADS, tm, 256), lambda i: (0, i, 0)),
                 row(256), row(512), row(256), row(32),
                 pl.BlockSpec((tm // tkb, KV_LORA, tkb), lambda i: (i, 0, 0)))
    return pl.pallas_call(
        _inproj_body,
        out_shape=out_shape,
        grid=(n // tm,),
        in_specs=[row(D_MODEL), _const_spec((1, D_MODEL)), _const_spec((D_MODEL, _C_END)),
                  _const_spec((1, Q_LORA)), _const_spec((1, KV_LORA)), _const_spec((Q_LORA, 768)),
                  _const_spec((MLA_HEADS // 2, 128, 512)), tab, tab],
        out_specs=out_specs,
        compiler_params=_params(("parallel",)),
        name="inproj",
    )(x, lw["norm_mix"], lw["w_in"], lw["q_norm"], lw["kv_norm"], lw["w_uq"], lw["w_ukp"], cos_tab, sin_tab)


def _pool_body(pre_ref, u_ref, w_ref, sc_ref, y_ref, halo_ref, *, pos0, tc, nb):
    t = pl.program_id(1)

    @pl.when(t == 0)
    def _():
        halo_ref[...] = pre_ref[...]

    lane = lax.broadcasted_iota(jnp.int32, (tc, D_POOL), 1)
    g0, g1, g2 = lane < 64, lane < 128, lane < 192
    wsz = jnp.where(g0, 2, jnp.where(g1, 4, jnp.where(g2, 8, 16)))
    pos = lax.broadcasted_iota(jnp.int32, (tc, D_POOL), 0) + (pos0 + t * tc)
    cnt = jnp.minimum(pos + 1, wsz).astype(F32)
    for r in range(nb):
        u = u_ref[r]
        full = jnp.concatenate([halo_ref[r], u], axis=0)
        a2 = full + pltpu.roll(full, 1, 0)
        a4 = a2 + pltpu.roll(a2, 2, 0)
        a8 = a4 + pltpu.roll(a4, 4, 0)
        a16 = a8 + pltpu.roll(a8, 8, 0)
        win = jnp.where(g0, a2[POOL_HALO:], jnp.where(g1, a4[POOL_HALO:], jnp.where(g2, a8[POOL_HALO:], a16[POOL_HALO:])))
        d = (win / cnt - u).astype(MXU_DTYPE)
        y_ref[r] = (_dot(d, w_ref[...]) * sc_ref[...]).astype(y_ref.dtype)
        halo_ref[r] = full[tc:, :]


def _pool(u, prefix, lw, pos0):
    b, t, _ = u.shape
    tc = _tile(t, POOL_ROWS)
    nb = _tile(b, min(POOL_BATCH_ROWS, max(1, POOL_ROWS // tc)))
    return pl.pallas_call(
        functools.partial(_pool_body, pos0=pos0, tc=tc, nb=nb),
        out_shape=jax.ShapeDtypeStruct((b, t, D_POOL), MXU_DTYPE),
        grid=(b // nb, t // tc),
        in_specs=[pl.BlockSpec((nb, POOL_HALO, D_POOL), lambda i, j: (i, 0, 0)),
                  pl.BlockSpec((nb, tc, D_POOL), lambda i, j: (i, j, 0)),
                  _const_spec((D_POOL, D_POOL)), _const_spec((1, D_POOL))],
        out_specs=pl.BlockSpec((nb, tc, D_POOL), lambda i, j: (i, j, 0)),
        scratch_shapes=[pltpu.VMEM((nb, POOL_HALO, D_POOL), F32)],
        compiler_params=_params(("parallel", "arbitrary")),
        name="pool",
    )(prefix, u, lw["pool_w"], lw["pool_scale"])


def _head_lane_masks(shape, width):
    lane = lax.broadcasted_iota(jnp.int32, shape, len(shape) - 1)
    return [(lane >= width * h) & (lane < width * (h + 1)) for h in range(MLSTM_HEADS)]


def _by_head(masks, vals):
    out = jnp.where(masks[0], vals[0], 0.0)
    for h in range(1, MLSTM_HEADS):
        out = jnp.where(masks[h], vals[h], out)
    return out


def _mlstm_body(qkv_ref, opre_ref, gates_ref, gbias_ref, nrm_ref, c0_ref, n0_ref, m0_ref,
                y_ref, cout_ref, nout_ref, mout_ref, c_ref, n_ref, m_ref, *, L, NB):
    t = pl.program_id(1)

    @pl.when(t == 0)
    def _():
        c_ref[...] = c0_ref[...]
        n_ref[...] = n0_ref[...]
        m_ref[...] = m0_ref[...]

    for r in range(NB):
        _mlstm_row(qkv_ref.at[r], opre_ref.at[r], gates_ref.at[r], gbias_ref, nrm_ref, y_ref.at[r], c_ref.at[r],
                   n_ref.at[r], m_ref.at[r], L=L)

    @pl.when(t == pl.num_programs(1) - 1)
    def _():
        cout_ref[...] = c_ref[...]
        nout_ref[...] = n_ref[...]
        mout_ref[...] = m_ref[...]


def _mlstm_row(qkv_ref, opre_ref, gates_ref, gbias_ref, nrm_ref, y_ref, c_ref, n_ref, m_ref, *, L):
    H, DH = MLSTM_HEADS, MLSTM_DH
    q = qkv_ref[:, 0:256]
    k = qkv_ref[:, 256:512]
    v = qkv_ref[:, 512:768]
    qt = q.astype(F32).T.astype(MXU_DTYPE)
    vt = v.astype(F32).T
    vtb = vt.astype(MXU_DTYPE)
    g = gates_ref[...] + gbias_ref[...]
    gt = g.T
    src_i = lax.broadcasted_iota(jnp.int32, (L, L), 0)
    tgt_i = lax.broadcasted_iota(jnp.int32, (L, L), 1)
    causal = src_i <= tgt_i
    b_cols = jnp.dot((tgt_i <= src_i).astype(F32), jax.nn.log_sigmoid(g), preferred_element_type=F32,
                     precision=lax.Precision.HIGHEST)
    b_rows = jnp.dot(jax.nn.log_sigmoid(gt), causal.astype(F32), preferred_element_type=F32,
                     precision=lax.Precision.HIGHEST)

    m_prev_all = m_ref[...]
    masks = _head_lane_masks((L, D_MLSTM), DH)
    zero_k = jnp.zeros_like(k)
    ct = c_ref[...]
    inter_num = _dot(ct.astype(MXU_DTYPE), qt)
    head_row = lax.broadcasted_iota(jnp.int32, (8, D_MLSTM), 0)
    head_lane = lax.broadcasted_iota(jnp.int32, (8, D_MLSTM), 1) // DH
    n8 = jnp.where(head_row == head_lane, jnp.broadcast_to(n_ref[...], (8, D_MLSTM)), 0.0)
    qn = _dot(n8.astype(MXU_DTYPE), qt)

    hn_parts, vw_parts, wk_rows, dec_11, mnew_11 = [], [], [], [], []
    for h in range(H):
        rs = slice(DH * h, DH * (h + 1))
        b_row = b_rows[4 + h:5 + h, :]
        ig_row = gt[h:h + 1, :]
        m_prev = m_prev_all[:, h:h + 1]
        src_col = b_cols[:, 4 + h:5 + h] - g[:, h:h + 1]
        logw = jnp.where(causal, b_row - src_col, -jnp.inf)
        inter = b_row + m_prev
        m_t = jnp.maximum(inter, jnp.max(logw, axis=0, keepdims=True))
        w = jnp.exp(logw - m_t)
        a = jnp.exp(inter - m_t)
        st = _dot_nt(jnp.where(masks[h], k, zero_k), q) * w
        num = _dot(vtb[rs, :], st.astype(MXU_DTYPE)) + a * inter_num[rs, :]
        den = jnp.sum(st, axis=0, keepdims=True) + a * qn[h:h + 1, :]
        hm = num / jnp.maximum(jnp.abs(den), jnp.exp(-m_t))
        ms = jnp.mean(hm * hm, axis=0, keepdims=True)
        hn_parts.append(hm * lax.rsqrt(ms + EPS))
        m_new = m_t[:, L - 1:L]
        b_last = b_row[:, L - 1:L]
        wk = jnp.exp(b_last - b_row + ig_row - m_new)
        mnew_11.append(m_new)
        dec_11.append(jnp.exp(b_last + m_prev - m_new))
        wk_rows.append(wk)
        vw_parts.append(vt[rs, :] * wk)

    hn = jnp.concatenate(hn_parts, axis=0) * nrm_ref[:, 0:L]
    y_ref[...] = (hn * jax.nn.sigmoid(opre_ref[...].T)).T.astype(y_ref.dtype)

    c_add = _dot(jnp.concatenate(vw_parts, axis=0).astype(MXU_DTYPE), k)
    wk8 = jnp.concatenate(wk_rows + [jnp.zeros((8 - H, L), F32)], axis=0).astype(MXU_DTYPE)
    n_add8 = _dot(wk8, k)
    masks_sq = _head_lane_masks((D_MLSTM, D_MLSTM), MLSTM_DH)
    rows_sq = lax.broadcasted_iota(jnp.int32, (D_MLSTM, D_MLSTM), 0)
    diag = jnp.zeros((D_MLSTM, D_MLSTM), jnp.bool_)
    for h in range(H):
        diag = diag | (masks_sq[h] & (rows_sq >= MLSTM_DH * h) & (rows_sq < MLSTM_DH * (h + 1)))
    lane1 = _head_lane_masks((1, D_MLSTM), MLSTM_DH)
    dec_lane = _by_head(lane1, dec_11)
    c_new = dec_lane * ct + jnp.where(diag, c_add, 0.0)
    n_new = dec_lane * n_ref[...] + _by_head(lane1, [n_add8[h:h + 1, :] for h in range(H)])
    lane_m = lax.broadcasted_iota(jnp.int32, (1, 128), 1)
    m_new_all = m_prev_all
    for h in range(H):
        m_new_all = jnp.where(lane_m == h, mnew_11[h], m_new_all)
    c_ref[...] = c_new
    n_ref[...] = n_new
    m_ref[...] = m_new_all


def _mlstm(qkv, opre, gates, lw, c0, n0, m0, L):
    b, t, _ = qkv.shape
    NB = _tile(b, MLSTM_ROWS if L == MLSTM_CHUNK else MLSTM_ROWS_SHORT)
    seq = lambda w: pl.BlockSpec((NB, L, w), lambda i, j: (i, j, 0))
    st = lambda r, w: pl.BlockSpec((NB, r, w), lambda i, j: (i, 0, 0))
    return pl.pallas_call(
        functools.partial(_mlstm_body, L=L, NB=NB),
        out_shape=(jax.ShapeDtypeStruct((b, t, D_MLSTM), MXU_DTYPE),
                   jax.ShapeDtypeStruct((b, D_MLSTM, D_MLSTM), F32),
                   jax.ShapeDtypeStruct((b, 1, D_MLSTM), F32),
                   jax.ShapeDtypeStruct((b, 1, 128), F32)),
        grid=(b // NB, t // L),
        in_specs=[seq(768), seq(256), seq(128), _const_spec((1, 128)), _const_spec((D_MLSTM, MLSTM_CHUNK)),
                  st(D_MLSTM, D_MLSTM), st(1, D_MLSTM), st(1, 128)],
        out_specs=(seq(256), st(D_MLSTM, D_MLSTM), st(1, D_MLSTM), st(1, 128)),
        scratch_shapes=[pltpu.VMEM((NB, D_MLSTM, D_MLSTM), F32), pltpu.VMEM((NB, 1, D_MLSTM), F32),
                        pltpu.VMEM((NB, 1, 128), F32)],
        compiler_params=_params(("parallel", "arbitrary")),
        name="mlstm",
    )(qkv, opre, gates, lw["gbias"], lw["mlstm_norm_t"], c0, n0, m0)


def _head_proj(o_heads, wuvp_ref, y_ref):
    for p in range(MLA_HEADS // 2):
        op = jnp.concatenate([o_heads(2 * p), o_heads(2 * p + 1)], axis=1).astype(MXU_DTYPE)
        y_ref[:, 128 * p:128 * p + 128] = _dot(op, wuvp_ref[p]).astype(y_ref.dtype)


ATTN_TQ = 256


def _attn_body(qlat_ref, r_ref, k_ref, kt_ref, wuvt_ref, y_ref, q2_ref, m_ref, l_ref, acc_ref, *, tq):
    i = pl.program_id(1)
    r = r_ref[...]
    lane = lax.broadcasted_iota(jnp.int32, (tq, 256), 1) & 127
    zero_r = jnp.zeros_like(r)
    for h in range(MLA_HEADS):
        q2_ref[h * tq:(h + 1) * tq, 0:256] = qlat_ref[h]
        q2_ref[h * tq:(h + 1) * tq, 256:512] = jnp.where((lane >= 16 * h) & (lane < 16 * h + 16), r, zero_r)
    m_ref[...] = jnp.full(m_ref.shape, -jnp.inf, F32)
    l_ref[...] = jnp.zeros(l_ref.shape, F32)
    acc_ref[...] = jnp.zeros(acc_ref.shape, F32)
    nq = MLA_HEADS * tq

    def block(j, nblk, masked):
        kb = k_ref[pl.ds(pl.multiple_of(j * tq, tq), nblk * tq), :]
        vt = kt_ref[j] if nblk == 1 else jnp.concatenate([kt_ref[j + c] for c in range(nblk)], axis=1)
        st = _dot_nt(kb, q2_ref[...]) * SCORE_SCALE_LOG2
        if masked:
            kpos = lax.broadcasted_iota(jnp.int32, (nblk * tq, nq), 0)
            qpos = (lax.broadcasted_iota(jnp.int32, (nblk * tq, nq), 1) & (tq - 1)) + (nblk - 1) * tq
            st = jnp.where(kpos <= qpos, st, -jnp.inf)
        m_old = m_ref[...]
        m_new = jnp.maximum(m_old, jnp.max(st, axis=0, keepdims=True))
        alpha = jnp.exp2(m_old - m_new)
        p = jnp.exp2(st - m_new)
        l_ref[...] = alpha * l_ref[...] + jnp.sum(p, axis=0, keepdims=True)
        acc_ref[...] = alpha * acc_ref[...] + _dot(vt, p.astype(MXU_DTYPE))
        m_ref[...] = m_new

    def body(j, carry):
        block(2 * j, 2, False)
        return carry

    lax.fori_loop(0, i // 2, body, 0)

    @pl.when(i % 2 == 1)
    def _():
        block(i - 1, 2, True)

    @pl.when(i % 2 == 0)
    def _():
        block(i, 1, True)

    o = (acc_ref[...] / l_ref[...]).astype(MXU_DTYPE)
    yt = [_dot(wuvt_ref[h], o[:, h * tq:(h + 1) * tq]) for h in range(MLA_HEADS)]
    y_ref[...] = jnp.concatenate(yt, axis=0).T.astype(y_ref.dtype)


def _attn_prompt(qlat, r, kp, kt, lw, b, t):
    tq = ATTN_TQ
    nq = t // tq
    return pl.pallas_call(
        functools.partial(_attn_body, tq=tq),
        out_shape=jax.ShapeDtypeStruct((b * t, D_MLA), MXU_DTYPE),
        grid=(b, nq),
        in_specs=[pl.BlockSpec((MLA_HEADS, tq, 256), lambda i, j: (0, i * nq + j, 0)),
                  pl.BlockSpec((tq, 256), lambda i, j: (i * nq + j, 0)),
                  pl.BlockSpec((t, 512), lambda i, j: (i, 0)),
                  pl.BlockSpec((nq, KV_LORA, tq), lambda i, j: (i, 0, 0)),
                  _const_spec((MLA_HEADS, V_DIM, KV_LORA))],
        out_specs=pl.BlockSpec((tq, D_MLA), lambda i, j: (i * nq + j, 0)),
        scratch_shapes=[pltpu.VMEM((MLA_HEADS * tq, 512), MXU_DTYPE), pltpu.VMEM((1, MLA_HEADS * tq), F32),
                        pltpu.VMEM((1, MLA_HEADS * tq), F32), pltpu.VMEM((KV_LORA, MLA_HEADS * tq), F32)],
        compiler_params=_params(("parallel", "arbitrary")),
        name="attn_prompt",
    )(qlat, r, kp, kt, lw["w_uvt"])


NEW_PAD = 16
SAMPLE_SLOTS = 4


def _attn_sample_body(pt_ref, ql_ref, qr_ref, cn_ref, kn_ref, lat_hbm, krt_hbm, o_ref,
                      lat_buf, kr_buf, sem_lat, sem_kr, *, G, T, layer, n_pages):
    b = pl.program_id(0)
    n_rows = pl.num_programs(0)
    n_groups = n_pages // G

    def group_copies(row, grp, slot):
        out = []
        for g in range(G):
            page = pt_ref[row * n_pages + grp * G + g]
            out.append(pltpu.make_async_copy(lat_hbm.at[layer, page], lat_buf.at[slot, g], sem_lat.at[slot]))
            out.append(pltpu.make_async_copy(krt_hbm.at[layer, page], kr_buf.at[slot, g], sem_kr.at[slot]))
        return out

    def start(row, grp, slot):
        for c in group_copies(row, grp, slot):
            c.start()

    def wait(row, grp, slot):
        for c in group_copies(row, grp, slot):
            c.wait()

    NS = SAMPLE_SLOTS

    @pl.when(b == 0)
    def _():
        for g in range(NS - 1):
            start(0, g, g)

    def start_ahead(g):
        nxt = g + NS - 1
        if nxt < n_groups:
            start(b, nxt, nxt % NS)
        else:
            @pl.when(b + 1 < n_rows)
            def _():
                start(b + 1, nxt - n_groups, nxt % NS)

    ql = ql_ref[...]
    qr = qr_ref[...]

    def values(slot):
        return [lat_buf[slot, g].astype(MXU_DTYPE) for g in range(G)]

    def scores(slot):
        cls = values(slot)
        return jnp.concatenate([_dot_nt(ql, cls[g]) + _dot(qr, kr_buf[slot, g].astype(MXU_DTYPE))
                                for g in range(G)], axis=1) * ATTN_SCALE

    def update(stats, s, vals):
        m_old, l_old, acc = stats
        m_new = jnp.maximum(m_old, jnp.max(s, axis=1, keepdims=True))
        alpha = jnp.exp(m_old - m_new)
        pf = jnp.exp(s - m_new)
        p = pf.astype(MXU_DTYPE)
        acc = alpha * acc
        off = 0
        for vb in vals:
            acc = acc + _dot(p[:, off:off + vb.shape[0]], vb)
            off += vb.shape[0]
        return m_new, alpha * l_old + jnp.sum(pf, axis=1, keepdims=True), acc

    rows = MLA_HEADS * T
    stats = (jnp.full((rows, 1), -jnp.inf, F32), jnp.zeros((rows, 1), F32), jnp.zeros((rows, KV_LORA), F32))
    wait(b, 0, 0)
    s_cur = scores(0)
    for grp in range(n_groups):
        if grp + 1 < n_groups:
            wait(b, grp + 1, (grp + 1) % NS)
        start_ahead(grp)
        if grp + 1 < n_groups:
            s_next = scores((grp + 1) % NS)
        stats = update(stats, s_cur, values(grp % NS))
        if grp + 1 < n_groups:
            s_cur = s_next

    cn = cn_ref[...].astype(MXU_DTYPE)
    kn = kn_ref[...].astype(MXU_DTYPE)
    sn = (_dot_nt(ql, cn) + _dot_nt(qr, kn)) * ATTN_SCALE
    qpos = lax.broadcasted_iota(jnp.int32, sn.shape, 0) % T
    kpos = lax.broadcasted_iota(jnp.int32, sn.shape, 1)
    _, l_fin, acc_fin = update(stats, jnp.where(kpos <= qpos, sn, -jnp.inf), [cn])
    o_ref[...] = acc_fin / l_fin


def _attn_sample(ql, qr, cn, kn, cache_latent, cache_krope_t, page_table, layer, G):
    db, rows, _ = ql.shape
    T = rows // MLA_HEADS
    n_pages = page_table.shape[1]
    page = cache_latent.shape[2]
    G = _tile(n_pages, G)
    NS = SAMPLE_SLOTS
    assert (n_pages // G) % NS == 0, "page groups per row must be a multiple of the number of VMEM slots"
    per_b = lambda r, w: pl.BlockSpec((None, r, w), lambda i, pt: (i, 0, 0))
    hbm = pl.BlockSpec(memory_space=pl.ANY)
    grid_spec = pltpu.PrefetchScalarGridSpec(
        num_scalar_prefetch=1,
        grid=(db,),
        in_specs=[per_b(rows, KV_LORA), per_b(rows, QK_ROPE), per_b(NEW_PAD, KV_LORA), per_b(NEW_PAD, QK_ROPE), hbm, hbm],
        out_specs=pl.BlockSpec((None, rows, KV_LORA), lambda i, pt: (i, 0, 0)),
        scratch_shapes=[pltpu.VMEM((NS, G, page, KV_LORA), F32), pltpu.VMEM((NS, G, QK_ROPE, page), F32),
                        pltpu.SemaphoreType.DMA((NS,)), pltpu.SemaphoreType.DMA((NS,))],
    )
    return pl.pallas_call(
        functools.partial(_attn_sample_body, G=G, T=T, layer=layer, n_pages=n_pages),
        out_shape=jax.ShapeDtypeStruct((db, rows, KV_LORA), F32),
        grid_spec=grid_spec,
        compiler_params=_params(("arbitrary",)),
        name="attn_sample",
    )(page_table.reshape(-1), ql, qr, cn, kn, cache_latent, cache_krope_t)


def _uvproj_body(o_ref, wuvp_ref, y_ref):
    _head_proj(lambda h: o_ref[h], wuvp_ref, y_ref)


def _uvproj(o, lw):
    n = o.shape[1]
    return pl.pallas_call(
        _uvproj_body,
        out_shape=jax.ShapeDtypeStruct((n, D_MLA), MXU_DTYPE),
        grid=(1,),
        in_specs=[pl.BlockSpec((MLA_HEADS, n, 256), lambda i: (0, 0, 0)), _const_spec((MLA_HEADS // 2, 512, 128))],
        out_specs=pl.BlockSpec((n, D_MLA), lambda i: (0, 0)),
        compiler_params=_params(("arbitrary",)),
        name="uvproj",
    )(o, lw["w_uvp"])


FFN_CHUNK = 256


def _ffn_body(x_ref, yp_ref, ym_ref, ya_ref, p_ref, wout_ref, nf_ref, wg_ref, wu_ref, wd_ref, pn_ref, wpg_ref,
              wpp_ref, fn_ref, xo_ref, *maybe_yo, final):
    mix = (_dot(yp_ref[...], wout_ref[0:256, :]) + _dot(ym_ref[...], wout_ref[256:512, :])
           + _dot(ya_ref[...], wout_ref[512:1024, :]))
    x1 = x_ref[...] + mix
    hf = _rms(x1, nf_ref[...]).astype(MXU_DTYPE)
    acc = jnp.zeros_like(x1)
    for c in range(D_FF // FFN_CHUNK):
        sl = slice(c * FFN_CHUNK, (c + 1) * FFN_CHUNK)
        gate = _dot(hf, wg_ref[:, sl])
        up = _dot(hf, wu_ref[:, sl])
        acc = acc + _dot((gate * jax.nn.sigmoid(gate) * up).astype(MXU_DTYPE), wd_ref[sl, :])
    x2 = x1 + acc
    pg = jax.nn.sigmoid(_dot(_rms(x2, pn_ref[...]).astype(MXU_DTYPE), wpg_ref[...]))
    x3 = x2 + pg * _dot(p_ref[...].astype(MXU_DTYPE), wpp_ref[...])
    xo_ref[...] = x3
    if final:
        maybe_yo[0][...] = _rms(x3, fn_ref[...])


def _ffn(x, yp, ym, ya, p, layer, lw, final_norm, final):
    n = x.shape[0]
    tm = _tile(n, 512)
    row = lambda w: pl.BlockSpec((tm, w), lambda i: (i, 0))
    p_spec = pl.BlockSpec((None, tm, D_PLE), lambda i: (layer, i, 0))
    out_shape = [jax.ShapeDtypeStruct((n, D_MODEL), F32)]
    out_specs = [row(D_MODEL)]
    if final:
        out_shape.append(jax.ShapeDtypeStruct((n, D_MODEL), F32))
        out_specs.append(row(D_MODEL))
    return pl.pallas_call(
        functools.partial(_ffn_body, final=final),
        out_shape=tuple(out_shape),
        grid=(n // tm,),
        in_specs=[row(D_MODEL), row(D_POOL), row(D_MLSTM), row(D_MLA), p_spec,
                  _const_spec((D_MODEL, D_MODEL)), _const_spec((1, D_MODEL)),
                  _const_spec((D_MODEL, D_FF)), _const_spec((D_MODEL, D_FF)), _const_spec((D_FF, D_MODEL)),
                  _const_spec((1, D_MODEL)), _const_spec((D_MODEL, D_MODEL)), _const_spec((D_PLE, D_MODEL)),
                  _const_spec((1, D_MODEL))],
        out_specs=tuple(out_specs),
        compiler_params=_params(("parallel",)),
        name="ffn",
    )(x, yp, ym, ya, p, lw["w_out"], lw["norm_ffn"], lw["w_gate"], lw["w_up"], lw["w_down"], lw["ple_norm"],
      lw["w_ple_gate"], lw["w_ple_proj"], final_norm)


def _block_diag(blocks):
    n = len(blocks)
    r, c = blocks[0].shape
    out = jnp.zeros((n * r, n * c), blocks[0].dtype)
    for i, blk in enumerate(blocks):
        out = out.at[i * r:(i + 1) * r, i * c:(i + 1) * c].set(blk)
    return out


def _pack_layer(i, norm_mix, w_in, b_igate, b_fgate, pool_w, pool_scale, mlstm_norm, q_norm, w_uq, kv_norm, w_uk,
                w_uv, w_out, norm_ffn, w_gate, w_up, w_down, ple_norm, w_ple_gate, w_ple_proj):
    wi = w_in[i]
    kr = wi[:, 1928:1960]
    packed = jnp.concatenate([
        wi[:, 256:1024], wi[:, 1024:1280], wi[:, 0:256], wi[:, 1288:1672], wi[:, 1672:1928],
        wi[:, 1280:1288], jnp.zeros((D_MODEL, 120), wi.dtype),
        jnp.tile(kr[:, 0:16], (1, MLA_HEADS)), jnp.tile(kr[:, 16:32], (1, MLA_HEADS))], axis=1)
    wq = w_uq[i]
    w_uq_p = jnp.concatenate([wq[:, :, 0:64].reshape(Q_LORA, 512), wq[:, :, 64:80].reshape(Q_LORA, 128),
                              wq[:, :, 80:96].reshape(Q_LORA, 128)], axis=1)
    wk = w_uk[i]
    w_ukp = jnp.stack([_block_diag([wk[:, 2 * p, :].T, wk[:, 2 * p + 1, :].T]) for p in range(MLA_HEADS // 2)])
    wv = w_uv[i]
    w_uvp = jnp.stack([_block_diag([wv[:, 2 * p, :], wv[:, 2 * p + 1, :]]) for p in range(MLA_HEADS // 2)])
    w_uvt = jnp.transpose(wv, (1, 2, 0))
    gbias = jnp.concatenate([b_igate[i], b_fgate[i], jnp.zeros((120,), F32)]).reshape(1, 128)
    mx = lambda a: a.astype(MXU_DTYPE)
    return dict(
        norm_mix=norm_mix[i].reshape(1, -1), w_in=mx(packed), q_norm=q_norm[i].reshape(1, -1),
        kv_norm=kv_norm[i].reshape(1, -1), w_uq=mx(w_uq_p), w_ukp=mx(w_ukp), w_uvp=mx(w_uvp), w_uvt=mx(w_uvt),
        pool_w=mx(_block_diag([pool_w[i, g] for g in range(4)])), pool_scale=pool_scale[i].reshape(1, -1),
        gbias=gbias, mlstm_norm_t=jnp.broadcast_to(mlstm_norm[i][:, None], (D_MLSTM, MLSTM_CHUNK)),
        w_out=mx(w_out[i]), norm_ffn=norm_ffn[i].reshape(1, -1), w_gate=mx(w_gate[i]), w_up=mx(w_up[i]),
        w_down=mx(w_down[i]), ple_norm=ple_norm[i].reshape(1, -1), w_ple_gate=mx(w_ple_gate[i]),
        w_ple_proj=mx(w_ple_proj[i]))


def _rope_tables(pos):
    inv = jnp.power(ROPE_THETA, -jnp.arange(0, QK_ROPE, 2, dtype=F32) / QK_ROPE)
    ang = pos.astype(F32)[:, None] * inv[None, :]
    return jnp.tile(jnp.cos(ang), (1, MLA_HEADS)), jnp.tile(jnp.sin(ang), (1, MLA_HEADS))


def _diag_blocks(cbd):
    return jnp.stack([jnp.swapaxes(cbd[:, 64 * h:64 * h + 64, 64 * h:64 * h + 64], 1, 2)
                      for h in range(MLSTM_HEADS)], axis=1)


def _pad_axis1(a, n, value=0.0):
    pad = [(0, 0)] * a.ndim
    pad[1] = (0, n - a.shape[1])
    return jnp.pad(a, pad, constant_values=value)


def _layer(x, p, layer, lw, cos_tab, sin_tab, b, t, pos0, prefix, c0, n0, m0, attend, final_norm, final):
    qkv, opre, upool, gates, qlat, r, kp, ckv, kr, kt = _inproj(x, lw, cos_tab, sin_tab)
    upool3 = upool.reshape(b, t, D_POOL)
    y_pool = _pool(_pad_axis1(upool3, -(-t // 8) * 8), prefix, lw, pos0)[:, :t].reshape(b * t, D_POOL)
    pool_state = jnp.concatenate([prefix[:, 1:], upool3], axis=1)[:, -POOL_STATE:]

    L = min(MLSTM_CHUNK, -(-t // 8) * 8)
    tp = -(-t // L) * L
    qkv3, opre3, gates3 = qkv.reshape(b, t, 768), opre.reshape(b, t, 256), gates.reshape(b, t, 128)
    if tp != t:
        lane = jnp.arange(128)
        padrow = jnp.where(lane < 4, NEG_BIG, jnp.where(lane < 8, 1e4, 0.0)).astype(F32)
        gates3 = jnp.concatenate([gates3, jnp.broadcast_to(padrow, (b, tp - t, 128))], axis=1)
        qkv3, opre3 = _pad_axis1(qkv3, tp), _pad_axis1(opre3, tp)
    y_ml, c_bd, n_new, m_new = _mlstm(qkv3, opre3, gates3, lw, c0, n0, m0, L)
    y_ml = y_ml[:, :t].reshape(b * t, D_MLSTM)

    y_mla = attend(qlat, r, kp, kt, ckv, kr)
    outs = _ffn(x, y_pool, y_ml, y_mla, p, layer, lw, final_norm, final)
    states = (ckv.reshape(b, t, KV_LORA), kr.reshape(b, t, QK_ROPE), pool_state, _diag_blocks(c_bd),
              n_new.reshape(b, MLSTM_HEADS, MLSTM_DH), m_new[:, 0, :MLSTM_HEADS])
    return outs, states


SAMPLE_PAGES_PER_STEP = 16


def kernel(x_prompt, x_sample, cache_latent, cache_krope, state_pool, state_C, state_n, state_m, page_table, p_prompt, p_sample, norm_mix, w_in, b_igate, b_fgate, pool_w, pool_scale, mlstm_norm, q_norm, w_uq, kv_norm, w_uk, w_uv, w_out, norm_ffn, w_gate, w_up, w_down, ple_norm, w_ple_gate, w_ple_proj, final_norm):
    bp, tp_, _ = x_prompt.shape
    bs, ts, _ = x_sample.shape
    depth = w_in.shape[0]
    past_len = page_table.shape[1] * cache_latent.shape[2]
    cos_p, sin_p = _rope_tables(jnp.arange(tp_))
    cos_s, sin_s = _rope_tables(past_len + jnp.arange(ts))
    cos_s, sin_s = jnp.tile(cos_s, (bs, 1)), jnp.tile(sin_s, (bs, 1))
    fin = final_norm.reshape(1, -1)
    cache_krope_t = jnp.swapaxes(cache_krope, 2, 3)

    xp = x_prompt.reshape(bp * tp_, D_MODEL)
    xs = x_sample.reshape(bs * ts, D_MODEL)
    st_p, st_s = [], []
    yp = ys = None
    for i in range(depth):
        lw = _pack_layer(i, norm_mix, w_in, b_igate, b_fgate, pool_w, pool_scale, mlstm_norm, q_norm, w_uq, kv_norm,
                         w_uk, w_uv, w_out, norm_ffn, w_gate, w_up, w_down, ple_norm, w_ple_gate, w_ple_proj)
        final = i == depth - 1

        def attend_s(qlat, r, kp, kt, ckv, kr):
            ql = qlat.reshape(MLA_HEADS, bs, ts, KV_LORA).transpose(1, 0, 2, 3).reshape(bs, MLA_HEADS * ts, KV_LORA)
            qr = r.reshape(bs, ts, 2, MLA_HEADS, 16).transpose(0, 3, 1, 2, 4).reshape(bs, MLA_HEADS * ts, QK_ROPE)
            cn = _pad_axis1(ckv.reshape(bs, ts, KV_LORA), NEW_PAD)
            kn = _pad_axis1(kr.reshape(bs, ts, QK_ROPE), NEW_PAD)
            o = _attn_sample(ql, qr, cn, kn, cache_latent, cache_krope_t, page_table, i, SAMPLE_PAGES_PER_STEP)
            o = o.reshape(bs, MLA_HEADS, ts, KV_LORA).transpose(1, 0, 2, 3).reshape(MLA_HEADS, bs * ts, KV_LORA)
            return _uvproj(o, lw)

        prefix_s = jnp.concatenate([jnp.zeros((bs, 1, D_POOL), F32), state_pool[i]], axis=1)
        c0 = jnp.concatenate([jnp.pad(jnp.swapaxes(state_C[i][:, h], 1, 2), ((0, 0), (0, 0), (64 * h, 192 - 64 * h)))
                              for h in range(MLSTM_HEADS)], axis=1)
        n0 = state_n[i].reshape(bs, 1, D_MLSTM)
        m0 = _pad_axis1(state_m[i], 128)[:, None, :]
        outs, ss = _layer(xs, p_sample.reshape(depth, bs * ts, D_PLE), i, lw, cos_s, sin_s, bs, ts, past_len,
                          prefix_s, c0, n0, m0, attend_s, fin, final)
        xs = outs[0]
        if final:
            ys = outs[1]

        def attend_p(qlat, r, kp, kt, ckv, kr):
            return _attn_prompt(qlat, r, kp, kt, lw, bp, tp_)

        outs, sp = _layer(xp, p_prompt.reshape(depth, bp * tp_, D_PLE), i, lw, cos_p, sin_p, bp, tp_, 0,
                          jnp.zeros((bp, POOL_HALO, D_POOL), F32), jnp.zeros((bp, D_MLSTM, D_MLSTM), F32),
                          jnp.zeros((bp, 1, D_MLSTM), F32), jnp.zeros((bp, 1, 128), F32), attend_p, fin, final)
        xp = outs[0]
        if final:
            yp = outs[1]
        st_p.append(sp)
        st_s.append(ss)

    stack = lambda sts, k: jnp.stack([s[k] for s in sts])
    return (yp.reshape(bp, tp_, D_MODEL), ys.reshape(bs, ts, D_MODEL),
            stack(st_p, 0), stack(st_p, 1), stack(st_p, 2), stack(st_p, 3), stack(st_p, 4), stack(st_p, 5),
            stack(st_s, 0), stack(st_s, 1), stack(st_s, 2), stack(st_s, 3), stack(st_s, 4), stack(st_s, 5))
```
